```python
import jax, jax.numpy as jnp
from jax import lax
import numpy as np

D_MODEL = 2048
BATCH = 8
SEQ = 2048
DEPTH = 2
DEC_BATCH = 32
DEC_SEQ = 4
PAST_LEN = 8192
PAGE_SIZE = 128

D_A = D_MODEL // 2
A_GROUPS = 8
A_CHUNK = 128
B_HEADS = 8
B_HEAD_DIM = (D_MODEL // 2) // B_HEADS
D_B = B_HEADS * B_HEAD_DIM
MOBA_BLOCK = 256
MOBA_TOPK = 3
Q_BLOCK = 128
D_C = D_MODEL
CONV_W = 31
N_AB = (DEPTH + 1) // 2
N_CONV = DEPTH // 2
EPS = 1e-6
NEG = -1e30

kernel_name = 'hybrid_chunkmlp_moba_conformer_decode_step'


def rmsnorm(x, g):
    xf = x.astype(jnp.float32)
    r = lax.rsqrt(jnp.mean(xf * xf, axis=-1, keepdims=True) + EPS)
    return (xf * r).astype(x.dtype) * g


def layernorm(x, g, b):
    xf = x.astype(jnp.float32)
    mu = jnp.mean(xf, axis=-1, keepdims=True)
    var = jnp.mean(jnp.square(xf - mu), axis=-1, keepdims=True)
    return ((xf - mu) * lax.rsqrt(var + EPS)).astype(x.dtype) * g + b


def adaln_mod(c, w, b):
    m = jax.nn.silu(c) @ w + b
    return jnp.split(m[:, None, :], 3, axis=-1)


def alibi_slopes():
    return jnp.asarray(np.array([2.0 ** (-8.0 * (h + 1) / B_HEADS) for h in range(B_HEADS)], np.float32))


def chunk_mix(u, v, ws, bs):
    bn, L, _ = u.shape
    n = min(L, A_CHUNK)
    nc = L // n
    w = ws[:, :n, :n] * jnp.tril(jnp.ones((n, n), ws.dtype))
    vr = v.reshape(bn, nc, n, A_GROUPS, D_A // A_GROUPS)
    mix = jnp.einsum('gts,bcsgd->bctgd', w, vr) + bs[:, :n].T[None, None, :, :, None]
    return u * mix.reshape(bn, L, D_A)


def moba_attend(q, kb, vb, t, slopes):
    H, NB = kb.shape[0], kb.shape[1]
    qf = q * (B_HEAD_DIM ** -0.5)
    kbar = jnp.mean(kb.astype(jnp.float32), axis=2)
    j = t // MOBA_BLOCK
    g = jnp.einsum('hqd,hnd->hqn', qf.astype(jnp.float32), kbar)
    past = jnp.arange(NB)[None, None, :] < j[None, :, None]
    g = jnp.where(past, g, NEG)
    _, top = lax.top_k(g, min(MOBA_TOPK, NB))
    top_valid = top < j[None, :, None]
    own = jnp.broadcast_to(j[None, :, None], (H, q.shape[1], 1)).astype(top.dtype)
    sel = jnp.concatenate([top, own], axis=-1)
    valid = jnp.concatenate([top_valid, jnp.ones(own.shape, bool)], axis=-1)
    hidx = jnp.arange(H)[:, None, None]
    ks = kb[hidx, sel]
    vs = vb[hidx, sel]
    pos = sel[..., None] * MOBA_BLOCK + jnp.arange(MOBA_BLOCK)
    dist = (t[None, :, None, None] - pos).astype(jnp.float32)
    s = jnp.einsum('hqd,hqnkd->hqnk', qf, ks).astype(jnp.float32) - slopes[:, None, None, None] * dist
    s = jnp.where(valid[..., None] & (dist >= 0), s, NEG)
    p = jax.nn.softmax(s.reshape(H, q.shape[1], -1), axis=-1).reshape(s.shape)
    return jnp.einsum('hqnk,hqnkd->hqd', p.astype(vs.dtype), vs)


def pad_to_block(x, axis):
    L = x.shape[axis]
    Lp = -(-L // MOBA_BLOCK) * MOBA_BLOCK
    pads = [(0, 0)] * x.ndim
    pads[axis] = (0, Lp - L)
    return jnp.pad(x, pads)


def moba_prompt(q, k, v, slopes):
    bn, S = q.shape[0], q.shape[1]
    kp, vp = pad_to_block(k, 1), pad_to_block(v, 1)
    nb = kp.shape[1] // MOBA_BLOCK
    kb = kp.reshape(bn, nb, MOBA_BLOCK, B_HEADS, B_HEAD_DIM).transpose(0, 3, 1, 2, 4)
    vb = vp.reshape(bn, nb, MOBA_BLOCK, B_HEADS, B_HEAD_DIM).transpose(0, 3, 1, 2, 4)
    nq = S // Q_BLOCK
    qb = q.reshape(bn, nq, Q_BLOCK, B_HEADS, B_HEAD_DIM).transpose(0, 1, 3, 2, 4)
    tq = jnp.arange(S, dtype=jnp.int32).reshape(nq, Q_BLOCK)

    def per_seq(args):
        qs, kbs, vbs = args
        return lax.map(lambda a: moba_attend(a[0], kbs, vbs, a[1], slopes), (qs, tq))

    out = lax.map(per_seq, (qb, kb, vb))
    return out.transpose(0, 1, 3, 2, 4).reshape(bn, S, D_B)


def moba_sample(q, k, v, cache_k, cache_v, page_table, slopes):
    db, ds = q.shape[0], q.shape[1]
    past = page_table.shape[1] * cache_k.shape[1]
    tq = past + jnp.arange(ds, dtype=jnp.int32)

    def per_seq(args):
        qs, kn, vn, pt = args
        kall = pad_to_block(jnp.concatenate([cache_k[pt].reshape(past, B_HEADS, B_HEAD_DIM), kn], axis=0), 0)
        vall = pad_to_block(jnp.concatenate([cache_v[pt].reshape(past, B_HEADS, B_HEAD_DIM), vn], axis=0), 0)
        nb = kall.shape[0] // MOBA_BLOCK
        kb = kall.reshape(nb, MOBA_BLOCK, B_HEADS, B_HEAD_DIM).transpose(2, 0, 1, 3)
        vb = vall.reshape(nb, MOBA_BLOCK, B_HEADS, B_HEAD_DIM).transpose(2, 0, 1, 3)
        return moba_attend(qs.transpose(1, 0, 2), kb, vb, tq, slopes)

    out = lax.map(per_seq, (q, k, v, page_table))
    return out.transpose(0, 2, 1, 3).reshape(db, ds, D_B)


def ab_project(h, w_in, ln_g, ln_b, ws, bs):
    widths = [D_A, D_A, D_A, D_B, D_B, D_B, D_B]
    idx = [int(i) for i in np.cumsum(widths)[:-1]]
    u_a, v_a, z_a, q_b, k_b, v_b, z_b = jnp.split(h @ w_in, idx, axis=-1)
    bn, L = h.shape[0], h.shape[1]
    vn = layernorm(jax.nn.gelu(v_a), ln_g, ln_b)
    y_a = chunk_mix(jax.nn.gelu(u_a), vn, ws, bs) * jax.nn.silu(z_a)
    shp = (bn, L, B_HEADS, B_HEAD_DIM)
    return y_a, q_b.reshape(shp), k_b.reshape(shp), v_b.reshape(shp), z_b, vn


def conv_mixer(h, w_in, conv_w, conv_b, ln_g, ln_b, w_out, prev):
    a, b, z = jnp.split(h @ w_in, 3, axis=-1)
    glu = a * jax.nn.sigmoid(b)
    full = jnp.concatenate([prev.astype(glu.dtype), glu], axis=1)
    y = lax.conv_general_dilated(full, conv_w[:, None, :].astype(full.dtype), window_strides=(1,), padding='VALID',
                                 dimension_numbers=('NWC', 'WIO', 'NWC'), feature_group_count=D_C) + conv_b
    y = jax.nn.silu(layernorm(y, ln_g, ln_b)) * jax.nn.silu(z)
    return y @ w_out, full[:, -(CONV_W - 1):]


def setup_inputs(seed: int = 0) -> dict:
    key = jax.random.key(seed)
    ks = jax.random.split(key, 32)
    f = jnp.float32
    n_pages = PAST_LEN // PAGE_SIZE
    n_phys = (DEC_BATCH * n_pages * 5) // 4
    nrm = lambda k, s, sc: jax.random.normal(k, s, f) * sc
    d_ab_in = 3 * D_A + 4 * D_B
    perm = jax.random.permutation(ks[5], n_phys)[:DEC_BATCH * n_pages]
    return {
        'x_prompt': nrm(ks[0], (BATCH, SEQ, D_MODEL), 1.0),
        'x_sample': nrm(ks[1], (DEC_BATCH, DEC_SEQ, D_MODEL), 1.0),
        'cache_b_k': nrm(ks[2], (N_AB, n_phys, PAGE_SIZE, B_HEADS, B_HEAD_DIM), 1.0),
        'cache_b_v': nrm(ks[3], (N_AB, n_phys, PAGE_SIZE, B_HEADS, B_HEAD_DIM), 1.0),
        'state_c_conv': nrm(ks[4], (N_CONV, DEC_BATCH, CONV_W - 1, D_C), 0.5),
        'page_table': perm.reshape(DEC_BATCH, n_pages).astype(jnp.int32),
        'c_prompt': nrm(ks[6], (BATCH, D_MODEL), 1.0),
        'c_sample': nrm(ks[7], (DEC_BATCH, D_MODEL), 1.0),
        'ada_w': nrm(ks[8], (DEPTH, D_MODEL, 3 * D_MODEL), 0.5 * D_MODEL ** -0.5),
        'ada_b': nrm(ks[9], (DEPTH, 3 * D_MODEL), 0.01),
        'norm_g': 1.0 + nrm(ks[10], (DEPTH, D_MODEL), 0.05),
        'ab_w_in': nrm(ks[11], (N_AB, D_MODEL, d_ab_in), D_MODEL ** -0.5),
        'a_ln_g': 1.0 + nrm(ks[12], (N_AB, D_A), 0.05),
        'a_ln_b': nrm(ks[13], (N_AB, D_A), 0.05),
        'a_ws': nrm(ks[14], (N_AB, A_GROUPS, A_CHUNK, A_CHUNK), A_CHUNK ** -0.5),
        'a_bs': 1.0 + nrm(ks[15], (N_AB, A_GROUPS, A_CHUNK), 0.1),
        'ab_w_out': nrm(ks[16], (N_AB, D_A + D_B, D_MODEL), (D_A + D_B) ** -0.5),
        'c_w_in': nrm(ks[17], (N_CONV, D_MODEL, 3 * D_C), D_MODEL ** -0.5),
        'c_conv_w': nrm(ks[18], (N_CONV, CONV_W, D_C), CONV_W ** -0.5),
        'c_conv_b': nrm(ks[19], (N_CONV, D_C), 0.01),
        'c_ln_g': 1.0 + nrm(ks[20], (N_CONV, D_C), 0.05),
        'c_ln_b': nrm(ks[21], (N_CONV, D_C), 0.05),
        'c_w_out': nrm(ks[22], (N_CONV, D_C, D_MODEL), D_C ** -0.5),
        'final_norm_g': 1.0 + nrm(ks[23], (D_MODEL,), 0.05),
    }


def reference(x_prompt, x_sample, cache_b_k, cache_b_v, state_c_conv, page_table, c_prompt, c_sample,
              ada_w, ada_b, norm_g, ab_w_in, a_ln_g, a_ln_b, a_ws, a_bs, ab_w_out,
              c_w_in, c_conv_w, c_conv_b, c_ln_g, c_ln_b, c_w_out, final_norm_g):
    slopes = alibi_slopes()
    xp, xs = x_prompt, x_sample
    k_p, v_p, k_s, v_s, av_s, cv_p, cv_s = [], [], [], [], [], [], []
    for layer in range(DEPTH):
        shp, scp, gtp = adaln_mod(c_prompt, ada_w[layer], ada_b[layer])
        shs, scs, gts = adaln_mod(c_sample, ada_w[layer], ada_b[layer])
        hp = rmsnorm(xp, norm_g[layer]) * (1.0 + scp) + shp
        hs = rmsnorm(xs, norm_g[layer]) * (1.0 + scs) + shs
        i = layer // 2
        if layer % 2 == 0:
            ya_p, qp, kp, vp, zp, _ = ab_project(hp, ab_w_in[i], a_ln_g[i], a_ln_b[i], a_ws[i], a_bs[i])
            ya_s, qs, kn, vn, zs, vas = ab_project(hs, ab_w_in[i], a_ln_g[i], a_ln_b[i], a_ws[i], a_bs[i])
            yb_p = moba_prompt(qp, kp, vp, slopes) * jax.nn.silu(zp)
            yb_s = moba_sample(qs, kn, vn, cache_b_k[i], cache_b_v[i], page_table, slopes) * jax.nn.silu(zs)
            yp = jnp.concatenate([ya_p, yb_p], axis=-1) @ ab_w_out[i]
            ys = jnp.concatenate([ya_s, yb_s], axis=-1) @ ab_w_out[i]
            k_p.append(kp); v_p.append(vp); k_s.append(kn); v_s.append(vn); av_s.append(vas)
        else:
            zero_prev = jnp.zeros((xp.shape[0], CONV_W - 1, D_C), hp.dtype)
            yp, sp = conv_mixer(hp, c_w_in[i], c_conv_w[i], c_conv_b[i], c_ln_g[i], c_ln_b[i], c_w_out[i], zero_prev)
            ys, ss = conv_mixer(hs, c_w_in[i], c_conv_w[i], c_conv_b[i], c_ln_g[i], c_ln_b[i], c_w_out[i], state_c_conv[i])
            cv_p.append(sp); cv_s.append(ss)
        xp = xp + gtp * yp
        xs = xs + gts * ys
    y_prompt = rmsnorm(xp, final_norm_g)
    y_sample = rmsnorm(xs, final_norm_g)
    return (y_prompt, y_sample, jnp.stack(k_p), jnp.stack(v_p), jnp.stack(k_s), jnp.stack(v_s),
            jnp.stack(av_s), jnp.stack(cv_p), jnp.stack(cv_s))
```

```python
import functools

import numpy as np
import jax
import jax.numpy as jnp
from jax import lax
from jax.experimental import pallas as pl
from jax.experimental.pallas import tpu as pltpu

F32 = jnp.float32
BF16 = jnp.bfloat16
HIGHEST = lax.Precision.HIGHEST

D_MODEL = 2048
DEPTH = 2
PAST_LEN = 8192
PAGE_SIZE = 128
D_A = 1024
A_GROUPS = 8
A_CHUNK = 128
B_HEADS = 8
B_HEAD_DIM = 128
D_B = B_HEADS * B_HEAD_DIM
MOBA_BLOCK = 256
MOBA_TOPK = 3
D_C = D_MODEL
CONV_W = 31
EPS = 1e-6
NEG = -1e30
QK_SCALE = B_HEAD_DIM ** -0.5

VMEM_LIMIT_BYTES = 52 * 1024 * 1024
SUBLANES = 8
LANES = 128

NT_DIMS = (((1,), (1,)), ((), ()))


def _params(*sem):
    return pltpu.CompilerParams(dimension_semantics=sem, vmem_limit_bytes=VMEM_LIMIT_BYTES)


def _div_pow2(x, d):
    shift = int(d).bit_length() - 1
    assert 1 << shift == d
    return lax.shift_right_logical(x, jnp.int32(shift))


def _sigmoid(x):
    return jax.nn.sigmoid(x)


def _silu(x):
    return x * _sigmoid(x)


def _gelu_tanh(x):
    c = np.float32(np.sqrt(2.0 / np.pi))
    return x * (0.5 * (1.0 + jnp.tanh(c * (x + 0.044715 * (x * x * x)))))


def _layernorm_rows(x, g, b):
    mu = jnp.mean(x, axis=-1, keepdims=True)
    xc = x - mu
    var = jnp.mean(xc * xc, axis=-1, keepdims=True)
    return (xc * lax.rsqrt(var + EPS)) * g + b


def _adaln_kernel(c_ref, w_ref, b_ref, o_ref):
    s = _silu(c_ref[...]).astype(BF16)
    o_ref[0] = jnp.dot(s, w_ref[0].astype(BF16), preferred_element_type=F32) + b_ref[0]


def _adaln(c_all, ada_w, ada_b):
    n = c_all.shape[0]
    tn = 1024
    return pl.pallas_call(
        _adaln_kernel,
        grid=(DEPTH, 3 * D_MODEL // tn),
        in_specs=[
            pl.BlockSpec((n, D_MODEL), lambda l, j: (0, 0)),
            pl.BlockSpec((1, D_MODEL, tn), lambda l, j: (l, 0, j)),
            pl.BlockSpec((1, 1, tn), lambda l, j: (l, 0, j)),
        ],
        out_specs=pl.BlockSpec((1, n, tn), lambda l, j: (l, 0, j)),
        out_shape=jax.ShapeDtypeStruct((DEPTH, n, 3 * D_MODEL), F32),
        compiler_params=_params("arbitrary", "arbitrary"),
        name="adaln",
    )(c_all, ada_w, ada_b.reshape(DEPTH, 1, 3 * D_MODEL))


class _Mod:
    def __init__(self, arr, index):
        self.arr = arr
        self.rows = arr.shape[1]
        self.index = index

    def spec(self, which, width=D_MODEL, col=None):
        if col is None:
            return pl.BlockSpec((1, self.rows, width), lambda i: (self.index(i, which), 0, 0))
        return pl.BlockSpec((1, self.rows, width), lambda j, i: (self.index(i, which), 0, j))


def _norm_mod_kernel(x_ref, g_ref, sc_ref, sh_ref, h_ref):
    x = x_ref[...]
    r = lax.rsqrt(jnp.mean(x * x, axis=-1, keepdims=True) + EPS)
    h = (x * r) * g_ref[...] * (1.0 + sc_ref[0]) + sh_ref[0]
    h_ref[...] = h.astype(h_ref.dtype)


def _norm_mod(x, g, mod, tm):
    rows = x.shape[0]
    return pl.pallas_call(
        _norm_mod_kernel,
        grid=(rows // tm,),
        in_specs=[
            pl.BlockSpec((tm, D_MODEL), lambda i: (i, 0)),
            pl.BlockSpec((1, D_MODEL), lambda i: (0, 0)),
            mod.spec(1),
            mod.spec(0),
        ],
        out_specs=pl.BlockSpec((tm, D_MODEL), lambda i: (i, 0)),
        out_shape=jax.ShapeDtypeStruct((rows, D_MODEL), BF16),
        compiler_params=_params("arbitrary"),
        name="norm_mod",
    )(x, g.reshape(1, D_MODEL), mod.arr, mod.arr)


def _final_norm_kernel(x_ref, g_ref, o_ref):
    x = x_ref[...]
    r = lax.rsqrt(jnp.mean(x * x, axis=-1, keepdims=True) + EPS)
    o_ref[...] = (x * r) * g_ref[...]


def _final_norm(x, g, tm):
    rows = x.shape[0]
    return pl.pallas_call(
        _final_norm_kernel,
        grid=(rows // tm,),
        in_specs=[pl.BlockSpec((tm, D_MODEL), lambda i: (i, 0)),
                  pl.BlockSpec((1, D_MODEL), lambda i: (0, 0))],
        out_specs=pl.BlockSpec((tm, D_MODEL), lambda i: (i, 0)),
        out_shape=jax.ShapeDtypeStruct((rows, D_MODEL), F32),
        compiler_params=_params("arbitrary"),
        name="final_norm",
    )(x, g.reshape(1, D_MODEL))


def _proj_seg_kernel(h_ref, w_ref, g_ref, b_ref, *rest, epilogue):
    o_ref, wbf_ref = rest[0], rest[-1]

    @pl.when(pl.program_id(0) == 0)
    def _():
        wbf_ref[...] = w_ref[...].astype(BF16)

    acc = jnp.dot(h_ref[...], wbf_ref[...], preferred_element_type=F32)
    if epilogue == "gelu":
        acc = _gelu_tanh(acc)
    elif epilogue == "gelu_ln":
        acc = _layernorm_rows(_gelu_tanh(acc), g_ref[...], b_ref[...])
    elif epilogue == "silu":
        acc = _silu(acc)
    o_ref[...] = acc
    if epilogue == "heads":
        oh_ref = rest[1]
        tm = acc.shape[0]
        for hd in range(B_HEADS):
            oh_ref[pl.ds(hd, tm, stride=B_HEADS), :] = acc[:, hd * B_HEAD_DIM:(hd + 1) * B_HEAD_DIM]


def _proj_seg(h, w, seg, epilogue, ln_g, ln_b, tm):
    rows = h.shape[0]
    tn = 1024
    out_specs = [pl.BlockSpec((tm, tn), lambda i: (i, 0))]
    out_shape = [jax.ShapeDtypeStruct((rows, tn), F32)]
    if epilogue == "heads":
        out_specs.append(pl.BlockSpec((tm * B_HEADS, B_HEAD_DIM), lambda i: (i, 0)))
        out_shape.append(jax.ShapeDtypeStruct((rows * B_HEADS, B_HEAD_DIM), F32))
    out = pl.pallas_call(
        functools.partial(_proj_seg_kernel, epilogue=epilogue),
        grid=(rows // tm,),
        in_specs=[
            pl.BlockSpec((tm, D_MODEL), lambda i: (i, 0)),
            pl.BlockSpec((D_MODEL, tn), lambda i: (0, seg)),
            pl.BlockSpec((1, tn), lambda i: (0, 0)),
            pl.BlockSpec((1, tn), lambda i: (0, 0)),
        ],
        out_specs=out_specs,
        out_shape=out_shape,
        scratch_shapes=[pltpu.VMEM((D_MODEL, tn), BF16)],
        compiler_params=_params("arbitrary"),
        name="proj_" + epilogue,
    )(h, w, ln_g.reshape(1, tn), ln_b.reshape(1, tn))
    return out if epilogue == "heads" else out[0]


def _a_mix_kernel(u_ref, v_ref, z_ref, w_ref, bst_ref, o_ref, *, seq_len, tm):
    row = lax.broadcasted_iota(jnp.int32, (A_CHUNK, A_CHUNK), 0)
    col = lax.broadcasted_iota(jnp.int32, (A_CHUNK, A_CHUNK), 1)
    keep = jnp.where(col <= row, 1.0, 0.0)
    if seq_len < A_CHUNK:
        keep = keep * jnp.where(_div_pow2(row, seq_len) == _div_pow2(col, seq_len), 1.0, 0.0)
    gw = D_A // A_GROUPS
    for g in range(A_GROUPS):
        wg = (w_ref[g] * keep).astype(BF16)
        bias = bst_ref[:, g:g + 1]
        for c in range(tm // A_CHUNK):
            rs = slice(c * A_CHUNK, (c + 1) * A_CHUNK)
            cs = slice(g * gw, (g + 1) * gw)
            mix = jnp.dot(wg, v_ref[rs, cs].astype(BF16), preferred_element_type=F32) + bias
            o_ref[rs, cs] = ((u_ref[rs, cs] * mix) * z_ref[rs, cs]).astype(o_ref.dtype)


def _a_mix(u, vn, z, ws, bs, seq_len, tm):
    rows = u.shape[0]
    n = min(seq_len, A_CHUNK)
    reps = A_CHUNK // n
    wmat = jnp.tile(ws[:, :n, :n], (1, reps, reps))
    bst = jnp.tile(bs[:, :n], (1, reps)).T
    spec = pl.BlockSpec((tm, D_A), lambda i: (i, 0))
    return pl.pallas_call(
        functools.partial(_a_mix_kernel, seq_len=n, tm=tm),
        grid=(rows // tm,),
        in_specs=[spec, spec, spec,
                  pl.BlockSpec((A_GROUPS, A_CHUNK, A_CHUNK), lambda i: (0, 0, 0)),
                  pl.BlockSpec((A_CHUNK, A_GROUPS), lambda i: (0, 0))],
        out_specs=spec,
        out_shape=jax.ShapeDtypeStruct((rows, D_A), BF16),
        compiler_params=_params("arbitrary"),
        name="a_mix",
    )(u, vn, z, wmat, bst)


def _rank_select(g, valid, index, axis, count):
    gm = jnp.where(valid, g, NEG)
    n = g.shape[axis]
    rank = jnp.zeros(g.shape, F32)
    for m in range(n):
        ref = gm[m:m + 1, :] if axis == 0 else gm[:, m:m + 1]
        beats = jnp.where(ref > gm, 1.0, jnp.where(ref == gm, jnp.where(m < index, 1.0, 0.0), 0.0))
        rank = rank + beats
    return jnp.where(valid, jnp.where(rank < float(count), 1.0, 0.0), 0.0)


def _moba_prompt_kernel(slope_ref, q_ref, k_ref, v_ref, z_ref, o_ref, kb_scr, vb_scr, s_scr, sel_scr, *, seq):
    nb = seq // MOBA_BLOCK
    slope = slope_ref[pl.program_id(1)]
    k = k_ref[0]
    kb_scr[...] = k.astype(BF16)
    vb_scr[...] = v_ref[0].astype(BF16)

    kbar = jnp.mean(k.reshape(nb, MOBA_BLOCK, B_HEAD_DIM), axis=1)
    kbar = jnp.concatenate([kbar, jnp.zeros((LANES - nb, B_HEAD_DIM), F32)], axis=0)
    qf = q_ref[0] * QK_SCALE
    gt = lax.dot_general(kbar, qf, NT_DIMS, precision=HIGHEST, preferred_element_type=F32)
    blk = lax.broadcasted_iota(jnp.int32, (nb, seq), 0)
    own = _div_pow2(lax.broadcasted_iota(jnp.int32, (nb, seq), 1), MOBA_BLOCK)
    sel = _rank_select(gt[0:nb, :], blk < own, blk, 0, MOBA_TOPK)
    sel = jnp.concatenate([sel, jnp.zeros((LANES - nb, seq), F32)], axis=0)
    sel_scr[...] = sel.T

    rc = (lax.broadcasted_iota(jnp.int32, (MOBA_BLOCK, MOBA_BLOCK), 0)
          - lax.broadcasted_iota(jnp.int32, (MOBA_BLOCK, MOBA_BLOCK), 1)).astype(F32)
    for jb in range(nb):
        rows = slice(jb * MOBA_BLOCK, (jb + 1) * MOBA_BLOCK)
        nk = (jb + 1) * MOBA_BLOCK
        qb = (q_ref[0, rows, :] * QK_SCALE).astype(BF16)
        s = lax.dot_general(qb, kb_scr[0:nk, :], NT_DIMS, preferred_element_type=F32)
        m = None
        for n in range(jb + 1):
            cols = slice(n * MOBA_BLOCK, (n + 1) * MOBA_BLOCK)
            dist = rc + float(MOBA_BLOCK * (jb - n))
            slab = s[:, cols] - slope * dist
            if n < jb:
                slab = jnp.where(sel_scr[rows, n:n + 1] > 0.5, slab, NEG)
            else:
                slab = jnp.where(rc >= 0.0, slab, NEG)
            s_scr[:, cols] = slab
            mx = jnp.max(slab, axis=-1, keepdims=True)
            m = mx if m is None else jnp.maximum(m, mx)
        p = jnp.exp(s_scr[:, 0:nk] - m)
        l = jnp.sum(p, axis=-1, keepdims=True)
        o = jnp.dot(p.astype(BF16), vb_scr[0:nk, :], preferred_element_type=F32)
        o_ref[0, rows, :] = ((o / l) * z_ref[0, rows, :]).astype(o_ref.dtype)


def _alibi_slopes():
    return jnp.asarray(np.array([2.0 ** (-8.0 * (h + 1) / B_HEADS) for h in range(B_HEADS)], np.float32))


def _moba_prompt(q, k, v, zs, batch, seq):
    shp = (batch, seq, D_B)
    spec = pl.BlockSpec((1, seq, B_HEAD_DIM), lambda b, h: (b, 0, h))
    out = pl.pallas_call(
        functools.partial(_moba_prompt_kernel, seq=seq),
        grid=(batch, B_HEADS),
        in_specs=[pl.BlockSpec(memory_space=pltpu.SMEM), spec, spec, spec, spec],
        out_specs=spec,
        out_shape=jax.ShapeDtypeStruct(shp, BF16),
        scratch_shapes=[pltpu.VMEM((seq, B_HEAD_DIM), BF16), pltpu.VMEM((seq, B_HEAD_DIM), BF16),
                        pltpu.VMEM((MOBA_BLOCK, seq), F32), pltpu.VMEM((seq, LANES), F32)],
        compiler_params=_params("arbitrary", "arbitrary"),
        name="moba_prompt",
    )(_alibi_slopes(), q.reshape(shp), k.reshape(shp), v.reshape(shp), zs.reshape(shp))
    return out.reshape(batch * seq, D_B)


PAGES_PER_STEP = 8


def _moba_sample_kernel(pt_ref, slope_ref, trow_ref, q_ref, kn_ref, vn_ref, z_ref, *rest,
                        n_new, n_pages):
    pg = PAGES_PER_STEP
    k_pages, v_pages = rest[:pg], rest[pg:2 * pg]
    o_ref = rest[2 * pg]
    qbd_scr, qbf_scr, cat_scr, s_scr, p_scr, ksum_scr, acc_scr, pnew_scr, l_scr = rest[2 * pg + 1:]
    nks = n_pages // pg
    n_blocks = n_pages * PAGE_SIZE // MOBA_BLOCK
    blocks_per_step = pg * PAGE_SIZE // MOBA_BLOCK
    rows = n_new * B_HEADS
    keys_per_step = pg * PAGE_SIZE
    i = pl.program_id(1)
    head_of_lane = _div_pow2(lax.broadcasted_iota(jnp.int32, (B_HEADS, D_B), 1), B_HEAD_DIM)
    diag = head_of_lane == lax.broadcasted_iota(jnp.int32, (B_HEADS, D_B), 0)
    slope = slope_ref[:, 0:1]
    trow = trow_ref[:, 0:1]

    @pl.when(i == 0)
    def _():
        tiles = []
        for qi in range(n_new):
            qrow = q_ref[0, qi:qi + 1, :] * QK_SCALE
            tiles.append(jnp.where(diag, jnp.broadcast_to(qrow, (B_HEADS, D_B)), 0.0))
        qbd = jnp.concatenate(tiles, axis=0)
        qbf_scr[...] = qbd
        qbd_scr[...] = qbd.astype(BF16)

    def load_pages(pages):
        for r in range(pg):
            for hd in range(B_HEADS):
                part = pages[r][0, pl.ds(hd, PAGE_SIZE, stride=B_HEADS), :]
                cat_scr[r * PAGE_SIZE:(r + 1) * PAGE_SIZE,
                        hd * B_HEAD_DIM:(hd + 1) * B_HEAD_DIM] = part.astype(BF16)

    @pl.when(i < nks)
    def _():
        load_pages(k_pages)
        pair = None
        for r in range(pg):
            ps = jnp.sum(k_pages[r][0].reshape(PAGE_SIZE, B_HEADS, B_HEAD_DIM), axis=0)
            if r % 2 == 0:
                pair = ps
            else:
                ksum_scr[i * blocks_per_step + r // 2] = pair + ps
        s_scr[i] = lax.dot_general(qbd_scr[...], cat_scr[...], NT_DIMS, preferred_element_type=F32)

    @pl.when(i == nks - 1)
    def _():
        kbar = jnp.concatenate([ksum_scr[:, hd, :] for hd in range(B_HEADS)], axis=1)
        kbar = kbar * (1.0 / MOBA_BLOCK)
        g = lax.dot_general(qbf_scr[...], kbar, NT_DIMS, precision=HIGHEST,
                            preferred_element_type=F32)
        blk = lax.broadcasted_iota(jnp.int32, (rows, n_blocks), 1)
        sel = _rank_select(g, blk >= 0, blk, 1, MOBA_TOPK)
        lane = lax.broadcasted_iota(jnp.int32, (rows, MOBA_BLOCK), 1).astype(F32)
        m = None
        for n in range(n_blocks):
            gi, a = divmod(n, blocks_per_step)
            cols = slice(a * MOBA_BLOCK, (a + 1) * MOBA_BLOCK)
            dist = trow - (float(n * MOBA_BLOCK) + lane)
            slab = s_scr[gi, :, cols] - slope * dist
            slab = jnp.where(sel[:, n:n + 1] > 0.5, slab, NEG)
            s_scr[gi, :, cols] = slab
            mx = jnp.max(slab, axis=-1, keepdims=True)
            m = mx if m is None else jnp.maximum(m, mx)
        knew = jnp.concatenate([kn_ref[0], jnp.zeros((LANES - n_new, D_B), F32)], axis=0).astype(BF16)
        s_new = lax.dot_general(qbd_scr[...], knew, NT_DIMS, preferred_element_type=F32)
        lane_new = lax.broadcasted_iota(jnp.int32, (rows, LANES), 1).astype(F32)
        dist_new = trow - (float(n_pages * PAGE_SIZE) + lane_new)
        s_new = jnp.where(dist_new >= 0.0, s_new - slope * dist_new, NEG)
        m = jnp.maximum(m, jnp.max(s_new, axis=-1, keepdims=True))
        l = jnp.zeros((rows, 1), F32)
        for gi in range(nks):
            p = jnp.exp(s_scr[gi] - m)
            l = l + jnp.sum(p, axis=-1, keepdims=True)
            p_scr[gi] = p.astype(BF16)
        pn = jnp.exp(s_new - m)
        l = l + jnp.sum(pn, axis=-1, keepdims=True)
        pnew_scr[...] = pn.astype(BF16)
        l_scr[...] = jnp.broadcast_to(l, (rows, LANES))
        acc_scr[...] = jnp.zeros((rows, D_B), F32)

    @pl.when(i >= nks)
    def _():
        load_pages(v_pages)
        acc_scr[...] += jnp.dot(p_scr[i - nks], cat_scr[...], preferred_element_type=F32)

    @pl.when(i == 2 * nks - 1)
    def _():
        vnew = jnp.concatenate([vn_ref[0], jnp.zeros((LANES - n_new, D_B), F32)], axis=0).astype(BF16)
        o = acc_scr[...] + jnp.dot(pnew_scr[...], vnew, preferred_element_type=F32)
        o = o / l_scr[:, 0:1]
        out_rows = []
        for qi in range(n_new):
            tile = o[qi * B_HEADS:(qi + 1) * B_HEADS, :]
            out_rows.append(jnp.sum(jnp.where(diag, tile, 0.0), axis=0, keepdims=True))
        o_ref[0] = jnp.concatenate(out_rows, axis=0) * z_ref[0]


def _moba_sample(q, kn, vn, zs, cache_k, cache_v, page_table, n_seq, n_new):
    n_pages = page_table.shape[1]
    n_phys = cache_k.shape[0]
    pg = PAGES_PER_STEP
    nks = n_pages // pg
    rows = n_new * B_HEADS
    past = n_pages * PAGE_SIZE
    shp = (n_seq, n_new, D_B)
    slope_rows = np.tile(np.array([2.0 ** (-8.0 * (h + 1) / B_HEADS) for h in range(B_HEADS)], np.float32),
                         n_new)
    t_rows = np.repeat(past + np.arange(n_new, dtype=np.float32), B_HEADS)
    slope_rows = jnp.asarray(np.tile(slope_rows[:, None], (1, LANES)))
    t_rows = jnp.asarray(np.tile(t_rows[:, None], (1, LANES)))

    const = pl.BlockSpec((rows, LANES), lambda s, i, pt: (0, 0))
    new = pl.BlockSpec((1, n_new, D_B), lambda s, i, pt: (s, 0, 0))

    page_rows = PAGE_SIZE * B_HEADS

    def k_page(r):
        return pl.BlockSpec((1, page_rows, B_HEAD_DIM),
                            lambda s, i, pt: (pt[s, jnp.minimum(i, nks - 1) * pg + r], 0, 0))

    def v_page(r):
        return pl.BlockSpec((1, page_rows, B_HEAD_DIM),
                            lambda s, i, pt: (pt[s, jnp.maximum(i - nks, 0) * pg + r], 0, 0))

    ck = cache_k.reshape(n_phys, page_rows, B_HEAD_DIM)
    cv = cache_v.reshape(n_phys, page_rows, B_HEAD_DIM)
    out = pl.pallas_call(
        functools.partial(_moba_sample_kernel, n_new=n_new, n_pages=n_pages),
        grid_spec=pltpu.PrefetchScalarGridSpec(
            num_scalar_prefetch=1,
            grid=(n_seq, 2 * nks),
            in_specs=[const, const, new, new, new, new]
                     + [k_page(r) for r in range(pg)] + [v_page(r) for r in range(pg)],
            out_specs=new,
            scratch_shapes=[
                pltpu.VMEM((rows, D_B), BF16),
                pltpu.VMEM((rows, D_B), F32),
                pltpu.VMEM((pg * PAGE_SIZE, D_B), BF16),
                pltpu.VMEM((nks, rows, pg * PAGE_SIZE), F32),
                pltpu.VMEM((nks, rows, pg * PAGE_SIZE), BF16),
                pltpu.VMEM((past // MOBA_BLOCK, B_HEADS, B_HEAD_DIM), F32),
                pltpu.VMEM((rows, D_B), F32),
                pltpu.VMEM((rows, LANES), BF16),
                pltpu.VMEM((rows, LANES), F32),
            ]),
        out_shape=jax.ShapeDtypeStruct(shp, F32),
        compiler_params=_params("arbitrary", "arbitrary"),
        name="moba_sample",
    )(page_table, slope_rows, t_rows, q.reshape(shp), kn.reshape(shp), vn.reshape(shp), zs.reshape(shp),
      *([ck] * pg), *([cv] * pg))
    return out.reshape(n_seq * n_new, D_B)


def _out_proj_kernel(ya_ref, yb_ref, w_ref, x_ref, gt_ref, o_ref, wbf_ref):
    @pl.when(pl.program_id(1) == 0)
    def _():
        wbf_ref[...] = w_ref[...].astype(BF16)

    half = ya_ref.shape[1]
    acc = jnp.dot(ya_ref[...].astype(BF16), wbf_ref[0:half, :], preferred_element_type=F32)
    acc = acc + jnp.dot(yb_ref[...].astype(BF16), wbf_ref[half:, :], preferred_element_type=F32)
    o_ref[...] = x_ref[...] + gt_ref[0] * acc


def _out_proj(ya, yb, ya_col, yb_col, w, x, mod, tm):
    rows = x.shape[0]
    tn = 1024
    half = D_MODEL // 2
    return pl.pallas_call(
        _out_proj_kernel,
        grid=(D_MODEL // tn, rows // tm),
        in_specs=[
            pl.BlockSpec((tm, half), lambda j, i: (i, ya_col)),
            pl.BlockSpec((tm, half), lambda j, i: (i, yb_col)),
            pl.BlockSpec((D_MODEL, tn), lambda j, i: (0, j)),
            pl.BlockSpec((tm, tn), lambda j, i: (i, j)),
            mod.spec(2, width=tn, col=True),
        ],
        out_specs=pl.BlockSpec((tm, tn), lambda j, i: (i, j)),
        out_shape=jax.ShapeDtypeStruct((rows, D_MODEL), F32),
        scratch_shapes=[pltpu.VMEM((D_MODEL, tn), BF16)],
        compiler_params=_params("arbitrary", "arbitrary"),
        name="out_proj",
    )(ya, yb, w, x, mod.arr)


def _conv_proj_kernel(h_ref, wa_ref, wb_ref, wz_ref, glu_ref, sz_ref, wbf_ref):
    @pl.when(pl.program_id(1) == 0)
    def _():
        wbf_ref[0] = wa_ref[...].astype(BF16)
        wbf_ref[1] = wb_ref[...].astype(BF16)
        wbf_ref[2] = wz_ref[...].astype(BF16)

    h = h_ref[...]
    a = jnp.dot(h, wbf_ref[0], preferred_element_type=F32)
    b = jnp.dot(h, wbf_ref[1], preferred_element_type=F32)
    glu_ref[...] = a * _sigmoid(b)
    sz_ref[...] = _silu(jnp.dot(h, wbf_ref[2], preferred_element_type=F32))


def _conv_proj(h, w, tm):
    rows = h.shape[0]
    tn = 512
    nj = D_C // tn
    out = pl.BlockSpec((tm, tn), lambda j, i: (i, j))
    return pl.pallas_call(
        _conv_proj_kernel,
        grid=(nj, rows // tm),
        in_specs=[
            pl.BlockSpec((tm, D_MODEL), lambda j, i: (i, 0)),
            pl.BlockSpec((D_MODEL, tn), lambda j, i: (0, j)),
            pl.BlockSpec((D_MODEL, tn), lambda j, i: (0, nj + j)),
            pl.BlockSpec((D_MODEL, tn), lambda j, i: (0, 2 * nj + j)),
        ],
        out_specs=[out, out],
        out_shape=[jax.ShapeDtypeStruct((rows, D_C), F32)] * 2,
        scratch_shapes=[pltpu.VMEM((3, D_MODEL, tn), BF16)],
        compiler_params=_params("arbitrary", "arbitrary"),
        name="conv_proj",
    )(h, w, w, w)


HALO = 32
CONV_LANES = 256


def _conv_ln_kernel(glu_ref, halo_ref, sz_ref, w_ref, cb_ref, g_ref, b_ref, o_ref, buf_scr, y_scr,
                    *, tiles_per_seq, tm):
    first = (pl.program_id(0) % tiles_per_seq) == 0

    @pl.when(first)
    def _():
        buf_scr[0:HALO, :] = jnp.zeros((HALO, D_C), F32)

    @pl.when(jnp.logical_not(first))
    def _():
        buf_scr[0:HALO, :] = halo_ref[...]

    buf_scr[HALO:HALO + tm, :] = glu_ref[...]

    off = HALO - (CONV_W - 1)
    taps = [[(a, SUBLANES * a + r - off) for a in range((CONV_W + off) // SUBLANES + 1)
             if 0 <= SUBLANES * a + r - off < CONV_W] for r in range(SUBLANES)]
    rowid = lax.broadcasted_iota(jnp.int32, (SUBLANES, CONV_LANES), 0)

    for lc in range(D_C // CONV_LANES):
        lanes = slice(lc * CONV_LANES, (lc + 1) * CONV_LANES)

        def z_block(u0, r):
            acc = None
            for a, k in taps[r]:
                term = w_ref[k:k + 1, lanes] * buf_scr[pl.ds(u0 + SUBLANES * a, SUBLANES), lanes]
                acc = term if acc is None else acc + term
            return acc

        def body(blk, carry):
            t0 = pl.multiple_of(blk * SUBLANES, SUBLANES)
            y = z_block(t0, 0)
            nxt = []
            for r in range(1, SUBLANES):
                zn = z_block(t0 + SUBLANES, r)
                mixed = jnp.where(rowid >= r, carry[r - 1], zn)
                y = y + pltpu.roll(mixed, SUBLANES - r, 0)
                nxt.append(zn)
            y_scr[pl.ds(t0, SUBLANES), lanes] = y
            return tuple(nxt)

        init = tuple(z_block(0, r) for r in range(1, SUBLANES))
        lax.fori_loop(0, tm // SUBLANES, body, init)

    y = _layernorm_rows(y_scr[...] + cb_ref[...], g_ref[...], b_ref[...])
    o_ref[...] = (_silu(y) * sz_ref[...]).astype(o_ref.dtype)


def _conv_ln(glu, sz, conv_w, conv_b, ln_g, ln_b, seq, tm):
    rows = glu.shape[0]
    vec = pl.BlockSpec((1, D_C), lambda i: (0, 0))
    tile = pl.BlockSpec((tm, D_C), lambda i: (i, 0))
    return pl.pallas_call(
        functools.partial(_conv_ln_kernel, tiles_per_seq=seq // tm, tm=tm),
        grid=(rows // tm,),
        in_specs=[tile,
                  pl.BlockSpec((HALO, D_C), lambda i: (jnp.maximum(i * (tm // HALO) - 1, 0), 0)),
                  tile,
                  pl.BlockSpec((CONV_W, D_C), lambda i: (0, 0)),
                  vec, vec, vec],
        out_specs=tile,
        out_shape=jax.ShapeDtypeStruct((rows, D_C), BF16),
        scratch_shapes=[pltpu.VMEM((HALO + tm, D_C), F32), pltpu.VMEM((tm, D_C), F32)],
        compiler_params=_params("arbitrary"),
        name="conv_ln",
    )(glu, glu, sz, conv_w, conv_b.reshape(1, D_C), ln_g.reshape(1, D_C), ln_b.reshape(1, D_C))


def _conv_ln_sample_kernel(st_ref, glu_ref, sz_ref, w_ref, cb_ref, g_ref, b_ref, o_ref, ns_ref, buf_scr,
                           *, n_new):
    hist = CONV_W - 1
    buf_scr[0:hist, :] = st_ref[0]
    buf_scr[hist:hist + n_new, :] = glu_ref[0]
    acc = jnp.zeros((n_new, D_C), F32) + cb_ref[...]
    for k in range(CONV_W):
        acc = acc + w_ref[k:k + 1, :] * buf_scr[k:k + n_new, :]
    y = _layernorm_rows(acc, g_ref[...], b_ref[...])
    o_ref[0] = _silu(y) * sz_ref[0]
    ns_ref[0] = buf_scr[n_new:n_new + hist, :]


def _conv_ln_sample(state, glu, sz, conv_w, conv_b, ln_g, ln_b, n_seq, n_new):
    hist = CONV_W - 1
    vec = pl.BlockSpec((1, D_C), lambda s: (0, 0))
    new = pl.BlockSpec((1, n_new, D_C), lambda s: (s, 0, 0))
    st = pl.BlockSpec((1, hist, D_C), lambda s: (s, 0, 0))
    y, ns = pl.pallas_call(
        functools.partial(_conv_ln_sample_kernel, n_new=n_new),
        grid=(n_seq,),
        in_specs=[st, new, new, pl.BlockSpec((CONV_W, D_C), lambda s: (0, 0)), vec, vec, vec],
        out_specs=[new, st],
        out_shape=[jax.ShapeDtypeStruct((n_seq, n_new, D_C), F32),
                   jax.ShapeDtypeStruct((n_seq, hist, D_C), F32)],
        scratch_shapes=[pltpu.VMEM((hist + n_new, D_C), F32)],
        compiler_params=_params("arbitrary"),
        name="conv_ln_sample",
    )(state, glu.reshape(n_seq, n_new, D_C), sz.reshape(n_seq, n_new, D_C), conv_w,
      conv_b.reshape(1, D_C), ln_g.reshape(1, D_C), ln_b.reshape(1, D_C))
    return y.reshape(n_seq * n_new, D_C), ns


def _trunk(x, mods, tm, seq, weights, attend, conv):
    (norm_g, ab_w_in, a_ln_g, a_ln_b, a_ws, a_bs, ab_w_out, c_w_in, c_w_out, final_norm_g) = weights
    tm_norm = min(tm, 512)
    h = _norm_mod(x, norm_g[0], mods[0], tm_norm)
    seg = lambda s, ep: _proj_seg(h, ab_w_in[0], s, ep, a_ln_g[0], a_ln_b[0], tm)
    u, vn, za = seg(0, "gelu"), seg(1, "gelu_ln"), seg(2, "silu")
    q, zb = seg(3, "none"), seg(6, "silu")
    (k, k_heads), (v, v_heads) = seg(4, "heads"), seg(5, "heads")
    ya = _a_mix(u, vn, za, a_ws[0], a_bs[0], seq, tm_norm)
    yb = attend(q, k, v, zb)
    x1 = _out_proj(ya, yb, 0, 0, ab_w_out[0], x, mods[0], tm)
    h1 = _norm_mod(x1, norm_g[1], mods[1], tm_norm)
    glu, sz = _conv_proj(h1, c_w_in[0], tm)
    y2, conv_state = conv(glu, sz)
    x2 = _out_proj(y2, y2, 0, 1, c_w_out[0], x1, mods[1], tm)
    return _final_norm(x2, final_norm_g, tm_norm), k_heads, v_heads, vn, conv_state


def kernel(x_prompt, x_sample, cache_b_k, cache_b_v, state_c_conv, page_table, c_prompt, c_sample,
           ada_w, ada_b, norm_g, ab_w_in, a_ln_g, a_ln_b, a_ws, a_bs, ab_w_out,
           c_w_in, c_conv_w, c_conv_b, c_ln_g, c_ln_b, c_w_out, final_norm_g):
    batch, seq, _ = x_prompt.shape
    n_seq, n_new, _ = x_sample.shape
    weights = (norm_g, ab_w_in, a_ln_g, a_ln_b, a_ws, a_bs, ab_w_out, c_w_in, c_w_out, final_norm_g)

    mod = _adaln(jnp.concatenate([c_prompt, c_sample], axis=0), ada_w, ada_b)
    mod = mod.reshape(DEPTH, batch + n_seq, 3, D_MODEL)
    tm_p = 512
    tiles_per_seq = seq // tm_p
    mods_p, mods_s = [], []
    for l in range(DEPTH):
        per_seq = mod[l, :batch].reshape(batch * 3, 1, D_MODEL)
        mods_p.append(_Mod(per_seq, lambda i, which: (i // tiles_per_seq) * 3 + which))
        per_row = jnp.repeat(mod[l, batch:], n_new, axis=0).transpose(1, 0, 2)
        mods_s.append(_Mod(per_row, lambda i, which: which))

    yp, kp, vp, _, glu_p = _trunk(
        x_prompt.reshape(batch * seq, D_MODEL), mods_p, tm_p, seq, weights,
        attend=lambda q, k, v, zb: _moba_prompt(q, k, v, zb, batch, seq),
        conv=lambda glu, sz: (_conv_ln(glu, sz, c_conv_w[0], c_conv_b[0], c_ln_g[0], c_ln_b[0], seq, tm_p), glu))
    ys, ks, vs, vas, conv_s = _trunk(
        x_sample.reshape(n_seq * n_new, D_MODEL), mods_s, n_seq * n_new, n_new, weights,
        attend=lambda q, k, v, zb: _moba_sample(q, k, v, zb, cache_b_k[0], cache_b_v[0], page_table,
                                               n_seq, n_new),
        conv=lambda glu, sz: _conv_ln_sample(state_c_conv[0], glu, sz, c_conv_w[0], c_conv_b[0],
                                             c_ln_g[0], c_ln_b[0], n_seq, n_new))

    hd = (B_HEADS, B_HEAD_DIM)
    conv_p = glu_p.reshape(batch, seq, D_C)[:, seq - (CONV_W - 1):, :]
    return (yp.reshape(batch, seq, D_MODEL), ys.reshape(n_seq, n_new, D_MODEL),
            kp.reshape(1, batch, seq, *hd), vp.reshape(1, batch, seq, *hd),
            ks.reshape(1, n_seq, n_new, *hd), vs.reshape(1, n_seq, n_new, *hd),
            vas.reshape(1, n_seq, n_new, D_A), conv_p[None], conv_s[None])
```

```python
import functools

import numpy as np
import jax
import jax.numpy as jnp
from jax import lax
from jax.experimental import pallas as pl
from jax.experimental.pallas import tpu as pltpu

F32 = jnp.float32
BF16 = jnp.bfloat16
HIGHEST = lax.Precision.HIGHEST

D_MODEL = 2048
DEPTH = 2
PAST_LEN = 8192
PAGE_SIZE = 128
D_A = 1024
A_GROUPS = 8
A_CHUNK = 128
B_HEADS = 8
B_HEAD_DIM = 128
D_B = B_HEADS * B_HEAD_DIM
MOBA_BLOCK = 256
MOBA_TOPK = 3
D_C = D_MODEL
CONV_W = 31
EPS = 1e-6
NEG = -1e30
QK_SCALE = B_HEAD_DIM ** -0.5

VMEM_LIMIT_BYTES = 56 * 1024 * 1024
SUBLANES = 8
LANES = 128
PROJ_TILE = 1024
ROW_TILE = 512

NT_DIMS = (((1,), (1,)), ((), ()))


def _params(*sem):
    return pltpu.CompilerParams(dimension_semantics=sem, vmem_limit_bytes=VMEM_LIMIT_BYTES)


def _div_pow2(x, d):
    shift = int(d).bit_length() - 1
    assert 1 << shift == d
    return lax.shift_right_logical(x, jnp.int32(shift))


def _sigmoid(x):
    return jax.nn.sigmoid(x)


def _silu(x):
    return x * _sigmoid(x)


def _gelu_tanh(x):
    c = np.float32(np.sqrt(2.0 / np.pi))
    return x * (0.5 * (1.0 + jnp.tanh(c * (x + 0.044715 * (x * x * x)))))


def _layernorm_rows(x, g, b):
    mu = jnp.mean(x, axis=-1, keepdims=True)
    xc = x - mu
    var = jnp.mean(xc * xc, axis=-1, keepdims=True)
    return (xc * lax.rsqrt(var + EPS)) * g + b


def _adaln_kernel(c_ref, w_ref, b_ref, o_ref):
    s = _silu(c_ref[...]).astype(BF16)
    o_ref[0] = jnp.dot(s, w_ref[0].astype(BF16), preferred_element_type=F32) + b_ref[0]


def _adaln(c_all, ada_w, ada_b):
    n = c_all.shape[0]
    tn = 1024
    return pl.pallas_call(
        _adaln_kernel,
        grid=(DEPTH, 3 * D_MODEL // tn),
        in_specs=[
            pl.BlockSpec((n, D_MODEL), lambda l, j: (0, 0)),
            pl.BlockSpec((1, D_MODEL, tn), lambda l, j: (l, 0, j)),
            pl.BlockSpec((1, 1, tn), lambda l, j: (l, 0, j)),
        ],
        out_specs=pl.BlockSpec((1, n, tn), lambda l, j: (l, 0, j)),
        out_shape=jax.ShapeDtypeStruct((DEPTH, n, 3 * D_MODEL), F32),
        compiler_params=_params("arbitrary", "arbitrary"),
        name="adaln",
    )(c_all, ada_w, ada_b.reshape(DEPTH, 1, 3 * D_MODEL))


class _Mod:
    def __init__(self, arr, seq_len=None):
        self.arr = arr
        self.rows = arr.shape[1]
        self.seq_len = seq_len

    def spec(self, which, tm):
        if self.seq_len is None:
            return pl.BlockSpec((1, self.rows, D_MODEL), lambda i: (which, 0, 0))
        assert self.seq_len % tm == 0
        tiles = self.seq_len // tm
        return pl.BlockSpec((1, 1, D_MODEL), lambda i: ((i // tiles) * 3 + which, 0, 0))


def _norm_mod_kernel(x_ref, g_ref, sc_ref, sh_ref, h_ref):
    x = x_ref[...]
    r = lax.rsqrt(jnp.mean(x * x, axis=-1, keepdims=True) + EPS)
    h = (x * r) * g_ref[...] * (1.0 + sc_ref[0]) + sh_ref[0]
    h_ref[...] = h.astype(h_ref.dtype)


def _norm_mod(x, g, mod, tm):
    rows = x.shape[0]
    return pl.pallas_call(
        _norm_mod_kernel,
        grid=(rows // tm,),
        in_specs=[
            pl.BlockSpec((tm, D_MODEL), lambda i: (i, 0)),
            pl.BlockSpec((1, D_MODEL), lambda i: (0, 0)),
            mod.spec(1, tm),
            mod.spec(0, tm),
        ],
        out_specs=pl.BlockSpec((tm, D_MODEL), lambda i: (i, 0)),
        out_shape=jax.ShapeDtypeStruct((rows, D_MODEL), BF16),
        compiler_params=_params("arbitrary"),
        name="norm_mod",
    )(x, g.reshape(1, D_MODEL), mod.arr, mod.arr)


def _proj_seg_kernel(h_ref, w_ref, g_ref, b_ref, *rest, epilogue):
    o_ref, wbf_ref = rest[0], rest[-1]

    @pl.when(pl.program_id(0) == 0)
    def _():
        wbf_ref[...] = w_ref[...].astype(BF16)

    acc = jnp.dot(h_ref[...], wbf_ref[...], preferred_element_type=F32)
    if epilogue == "gelu":
        acc = _gelu_tanh(acc)
    elif epilogue == "gelu_ln":
        acc = _layernorm_rows(_gelu_tanh(acc), g_ref[...], b_ref[...])
    elif epilogue == "silu":
        acc = _silu(acc)
    o_ref[...] = acc
    if epilogue == "heads":
        oh_ref = rest[1]
        tm = acc.shape[0]
        for hd in range(B_HEADS):
            oh_ref[pl.ds(hd, tm, stride=B_HEADS), :] = acc[:, hd * B_HEAD_DIM:(hd + 1) * B_HEAD_DIM]


def _proj_seg(h, w, seg, epilogue, ln_g, ln_b, tm):
    rows = h.shape[0]
    tn = 1024
    out_specs = [pl.BlockSpec((tm, tn), lambda i: (i, 0))]
    out_shape = [jax.ShapeDtypeStruct((rows, tn), F32)]
    if epilogue == "heads":
        out_specs.append(pl.BlockSpec((tm * B_HEADS, B_HEAD_DIM), lambda i: (i, 0)))
        out_shape.append(jax.ShapeDtypeStruct((rows * B_HEADS, B_HEAD_DIM), F32))
    out = pl.pallas_call(
        functools.partial(_proj_seg_kernel, epilogue=epilogue),
        grid=(rows // tm,),
        in_specs=[
            pl.BlockSpec((tm, D_MODEL), lambda i: (i, 0)),
            pl.BlockSpec((D_MODEL, tn), lambda i: (0, seg), pipeline_mode=pl.Buffered(1)),
            pl.BlockSpec((1, tn), lambda i: (0, 0)),
            pl.BlockSpec((1, tn), lambda i: (0, 0)),
        ],
        out_specs=out_specs,
        out_shape=out_shape,
        scratch_shapes=[pltpu.VMEM((D_MODEL, tn), BF16)],
        compiler_params=_params("arbitrary"),
        name="proj_" + epilogue,
    )(h, w, ln_g.reshape(1, tn), ln_b.reshape(1, tn))
    return out if epilogue == "heads" else out[0]


def _a_mix_kernel(u_ref, v_ref, z_ref, w_ref, bst_ref, o_ref, *, seq_len, tm):
    row = lax.broadcasted_iota(jnp.int32, (A_CHUNK, A_CHUNK), 0)
    col = lax.broadcasted_iota(jnp.int32, (A_CHUNK, A_CHUNK), 1)
    keep = jnp.where(col <= row, 1.0, 0.0)
    if seq_len < A_CHUNK:
        keep = keep * jnp.where(_div_pow2(row, seq_len) == _div_pow2(col, seq_len), 1.0, 0.0)
    gw = D_A // A_GROUPS
    for g in range(A_GROUPS):
        wg = (w_ref[g] * keep).astype(BF16)
        bias = bst_ref[:, g:g + 1]
        for c in range(tm // A_CHUNK):
            rs = slice(c * A_CHUNK, (c + 1) * A_CHUNK)
            cs = slice(g * gw, (g + 1) * gw)
            mix = jnp.dot(wg, v_ref[rs, cs].astype(BF16), preferred_element_type=F32) + bias
            o_ref[rs, cs] = ((u_ref[rs, cs] * mix) * z_ref[rs, cs]).astype(o_ref.dtype)


def _a_mix(u, vn, z, ws, bs, seq_len, tm):
    rows = u.shape[0]
    n = min(seq_len, A_CHUNK)
    reps = A_CHUNK // n
    wmat = jnp.tile(ws[:, :n, :n], (1, reps, reps))
    bst = jnp.tile(bs[:, :n], (1, reps)).T
    spec = pl.BlockSpec((tm, D_A), lambda i: (i, 0))
    return pl.pallas_call(
        functools.partial(_a_mix_kernel, seq_len=n, tm=tm),
        grid=(rows // tm,),
        in_specs=[spec, spec, spec,
                  pl.BlockSpec((A_GROUPS, A_CHUNK, A_CHUNK), lambda i: (0, 0, 0)),
                  pl.BlockSpec((A_CHUNK, A_GROUPS), lambda i: (0, 0))],
        out_specs=spec,
        out_shape=jax.ShapeDtypeStruct((rows, D_A), BF16),
        compiler_params=_params("arbitrary"),
        name="a_mix",
    )(u, vn, z, wmat, bst)


def _rank_select(g, valid, index, axis, count):
    gm = jnp.where(valid, g, NEG)
    n = g.shape[axis]
    rank = jnp.zeros(g.shape, F32)
    for m in range(n):
        ref = gm[m:m + 1, :] if axis == 0 else gm[:, m:m + 1]
        beats = jnp.where(ref > gm, 1.0, jnp.where(ref == gm, jnp.where(m < index, 1.0, 0.0), 0.0))
        rank = rank + beats
    return jnp.where(valid, jnp.where(rank < float(count), 1.0, 0.0), 0.0)


def _moba_prompt_kernel(slope_ref, q_ref, k_ref, v_ref, z_ref, o_ref, kb_scr, vb_scr, s_scr, sel_scr, *, seq):
    nb = seq // MOBA_BLOCK
    slope = slope_ref[pl.program_id(1)]
    k = k_ref[0]
    kb_scr[...] = k.astype(BF16)
    vb_scr[...] = v_ref[0].astype(BF16)

    kbar = jnp.mean(k.reshape(nb, MOBA_BLOCK, B_HEAD_DIM), axis=1)
    kbar = jnp.concatenate([kbar, jnp.zeros((LANES - nb, B_HEAD_DIM), F32)], axis=0)
    qf = q_ref[0] * QK_SCALE
    gt = lax.dot_general(kbar, qf, NT_DIMS, precision=HIGHEST, preferred_element_type=F32)
    blk = lax.broadcasted_iota(jnp.int32, (nb, seq), 0)
    own = _div_pow2(lax.broadcasted_iota(jnp.int32, (nb, seq), 1), MOBA_BLOCK)
    sel = _rank_select(gt[0:nb, :], blk < own, blk, 0, MOBA_TOPK)
    sel = jnp.concatenate([sel, jnp.zeros((LANES - nb, seq), F32)], axis=0)
    sel_scr[...] = sel.T

    rc = (lax.broadcasted_iota(jnp.int32, (MOBA_BLOCK, MOBA_BLOCK), 0)
          - lax.broadcasted_iota(jnp.int32, (MOBA_BLOCK, MOBA_BLOCK), 1)).astype(F32)
    for jb in range(nb):
        rows = slice(jb * MOBA_BLOCK, (jb + 1) * MOBA_BLOCK)
        nk = (jb + 1) * MOBA_BLOCK
        qb = (q_ref[0, rows, :] * QK_SCALE).astype(BF16)
        s = lax.dot_general(qb, kb_scr[0:nk, :], NT_DIMS, preferred_element_type=F32)
        m = None
        for n in range(jb + 1):
            cols = slice(n * MOBA_BLOCK, (n + 1) * MOBA_BLOCK)
            dist = rc + float(MOBA_BLOCK * (jb - n))
            slab = s[:, cols] - slope * dist
            if n < jb:
                slab = jnp.where(sel_scr[rows, n:n + 1] > 0.5, slab, NEG)
            else:
                slab = jnp.where(rc >= 0.0, slab, NEG)
            s_scr[:, cols] = slab
            mx = jnp.max(slab, axis=-1, keepdims=True)
            m = mx if m is None else jnp.maximum(m, mx)
        p = jnp.exp(s_scr[:, 0:nk] - m)
        l = jnp.sum(p, axis=-1, keepdims=True)
        o = jnp.dot(p.astype(BF16), vb_scr[0:nk, :], preferred_element_type=F32)
        o_ref[0, rows, :] = ((o / l) * z_ref[0, rows, :]).astype(o_ref.dtype)


def _alibi_slopes():
    return jnp.asarray(np.array([2.0 ** (-8.0 * (h + 1) / B_HEADS) for h in range(B_HEADS)], np.float32))


def _moba_prompt(q, k, v, zs, batch, seq):
    shp = (batch, seq, D_B)
    spec = pl.BlockSpec((1, seq, B_HEAD_DIM), lambda b, h: (b, 0, h))
    out = pl.pallas_call(
        functools.partial(_moba_prompt_kernel, seq=seq),
        grid=(batch, B_HEADS),
        in_specs=[pl.BlockSpec(memory_space=pltpu.SMEM), spec, spec, spec, spec],
        out_specs=spec,
        out_shape=jax.ShapeDtypeStruct(shp, BF16),
        scratch_shapes=[pltpu.VMEM((seq, B_HEAD_DIM), BF16), pltpu.VMEM((seq, B_HEAD_DIM), BF16),
                        pltpu.VMEM((MOBA_BLOCK, seq), F32), pltpu.VMEM((seq, LANES), F32)],
        compiler_params=_params("arbitrary", "arbitrary"),
        name="moba_prompt",
    )(_alibi_slopes(), q.reshape(shp), k.reshape(shp), v.reshape(shp), zs.reshape(shp))
    return out.reshape(batch * seq, D_B)


PAGES_PER_STEP = 8


def _moba_sample_kernel(pt_ref, slope_ref, trow_ref, q_ref, kn_ref, vn_ref, z_ref, *rest,
                        n_new, n_pages):
    pg = PAGES_PER_STEP
    k_pages, v_pages = rest[:pg], rest[pg:2 * pg]
    o_ref = rest[2 * pg]
    qbd_scr, qbf_scr, cat_scr, s_scr, p_scr, ksum_scr, acc_scr, pnew_scr, l_scr = rest[2 * pg + 1:]
    nks = n_pages // pg
    n_blocks = n_pages * PAGE_SIZE // MOBA_BLOCK
    blocks_per_step = pg * PAGE_SIZE // MOBA_BLOCK
    rows = n_new * B_HEADS
    keys_per_step = pg * PAGE_SIZE
    i = pl.program_id(1)
    head_of_lane = _div_pow2(lax.broadcasted_iota(jnp.int32, (B_HEADS, D_B), 1), B_HEAD_DIM)
    diag = head_of_lane == lax.broadcasted_iota(jnp.int32, (B_HEADS, D_B), 0)
    slope = slope_ref[:, 0:1]
    trow = trow_ref[:, 0:1]

    @pl.when(i == 0)
    def _():
        tiles = []
        for qi in range(n_new):
            qrow = q_ref[0, qi:qi + 1, :] * QK_SCALE
            tiles.append(jnp.where(diag, jnp.broadcast_to(qrow, (B_HEADS, D_B)), 0.0))
        qbd = jnp.concatenate(tiles, axis=0)
        qbf_scr[...] = qbd
        qbd_scr[...] = qbd.astype(BF16)

    def load_pages(pages):
        for r in range(pg):
            for hd in range(B_HEADS):
                part = pages[r][0, pl.ds(hd, PAGE_SIZE, stride=B_HEADS), :]
                cat_scr[r * PAGE_SIZE:(r + 1) * PAGE_SIZE,
                        hd * B_HEAD_DIM:(hd + 1) * B_HEAD_DIM] = part.astype(BF16)

    @pl.when(i < nks)
    def _():
        load_pages(k_pages)
        pair = None
        for r in range(pg):
            ps = jnp.sum(k_pages[r][0].reshape(PAGE_SIZE, B_HEADS, B_HEAD_DIM), axis=0)
            if r % 2 == 0:
                pair = ps
            else:
                ksum_scr[i * blocks_per_step + r // 2] = pair + ps
        s_scr[i] = lax.dot_general(qbd_scr[...], cat_scr[...], NT_DIMS, preferred_element_type=F32)

    @pl.when(i == nks - 1)
    def _():
        kbar = jnp.concatenate([ksum_scr[:, hd, :] for hd in range(B_HEADS)], axis=1)
        kbar = kbar * (1.0 / MOBA_BLOCK)
        g = lax.dot_general(qbf_scr[...], kbar, NT_DIMS, precision=HIGHEST,
                            preferred_element_type=F32)
        blk = lax.broadcasted_iota(jnp.int32, (rows, n_blocks), 1)
        sel = _rank_select(g, blk >= 0, blk, 1, MOBA_TOPK)
        lane = lax.broadcasted_iota(jnp.int32, (rows, MOBA_BLOCK), 1).astype(F32)
        m = None
        for n in range(n_blocks):
            gi, a = divmod(n, blocks_per_step)
            cols = slice(a * MOBA_BLOCK, (a + 1) * MOBA_BLOCK)
            dist = trow - (float(n * MOBA_BLOCK) + lane)
            slab = s_scr[gi, :, cols] - slope * dist
            slab = jnp.where(sel[:, n:n + 1] > 0.5, slab, NEG)
            s_scr[gi, :, cols] = slab
            mx = jnp.max(slab, axis=-1, keepdims=True)
            m = mx if m is None else jnp.maximum(m, mx)
        knew = jnp.concatenate([kn_ref[0], jnp.zeros((LANES - n_new, D_B), F32)], axis=0).astype(BF16)
        s_new = lax.dot_general(qbd_scr[...], knew, NT_DIMS, preferred_element_type=F32)
        lane_new = lax.broadcasted_iota(jnp.int32, (rows, LANES), 1).astype(F32)
        dist_new = trow - (float(n_pages * PAGE_SIZE) + lane_new)
        s_new = jnp.where(dist_new >= 0.0, s_new - slope * dist_new, NEG)
        m = jnp.maximum(m, jnp.max(s_new, axis=-1, keepdims=True))
        l = jnp.zeros((rows, 1), F32)
        for gi in range(nks):
            p = jnp.exp(s_scr[gi] - m)
            l = l + jnp.sum(p, axis=-1, keepdims=True)
            p_scr[gi] = p.astype(BF16)
        pn = jnp.exp(s_new - m)
        l = l + jnp.sum(pn, axis=-1, keepdims=True)
        pnew_scr[...] = pn.astype(BF16)
        l_scr[...] = jnp.broadcast_to(l, (rows, LANES))
        acc_scr[...] = jnp.zeros((rows, D_B), F32)

    @pl.when(i >= nks)
    def _():
        load_pages(v_pages)
        acc_scr[...] += jnp.dot(p_scr[i - nks], cat_scr[...], preferred_element_type=F32)

    @pl.when(i == 2 * nks - 1)
    def _():
        vnew = jnp.concatenate([vn_ref[0], jnp.zeros((LANES - n_new, D_B), F32)], axis=0).astype(BF16)
        o = acc_scr[...] + jnp.dot(pnew_scr[...], vnew, preferred_element_type=F32)
        o = o / l_scr[:, 0:1]
        out_rows = []
        for qi in range(n_new):
            tile = o[qi * B_HEADS:(qi + 1) * B_HEADS, :]
            out_rows.append(jnp.sum(jnp.where(diag, tile, 0.0), axis=0, keepdims=True))
        o_ref[0] = jnp.concatenate(out_rows, axis=0) * z_ref[0]


def _moba_sample(q, kn, vn, zs, cache_k, cache_v, page_table, n_seq, n_new):
    n_pages = page_table.shape[1]
    n_phys = cache_k.shape[0]
    pg = PAGES_PER_STEP
    nks = n_pages // pg
    rows = n_new * B_HEADS
    past = n_pages * PAGE_SIZE
    shp = (n_seq, n_new, D_B)
    slope_rows = np.tile(np.array([2.0 ** (-8.0 * (h + 1) / B_HEADS) for h in range(B_HEADS)], np.float32),
                         n_new)
    t_rows = np.repeat(past + np.arange(n_new, dtype=np.float32), B_HEADS)
    slope_rows = jnp.asarray(np.tile(slope_rows[:, None], (1, LANES)))
    t_rows = jnp.asarray(np.tile(t_rows[:, None], (1, LANES)))

    const = pl.BlockSpec((rows, LANES), lambda s, i, pt: (0, 0))
    new = pl.BlockSpec((1, n_new, D_B), lambda s, i, pt: (s, 0, 0))

    page_rows = PAGE_SIZE * B_HEADS

    def k_page(r):
        return pl.BlockSpec((1, page_rows, B_HEAD_DIM),
                            lambda s, i, pt: (pt[s, jnp.minimum(i, nks - 1) * pg + r], 0, 0))

    def v_page(r):
        return pl.BlockSpec((1, page_rows, B_HEAD_DIM),
                            lambda s, i, pt: (pt[s, jnp.maximum(i - nks, 0) * pg + r], 0, 0))

    ck = cache_k.reshape(n_phys, page_rows, B_HEAD_DIM)
    cv = cache_v.reshape(n_phys, page_rows, B_HEAD_DIM)
    out = pl.pallas_call(
        functools.partial(_moba_sample_kernel, n_new=n_new, n_pages=n_pages),
        grid_spec=pltpu.PrefetchScalarGridSpec(
            num_scalar_prefetch=1,
            grid=(n_seq, 2 * nks),
            in_specs=[const, const, new, new, new, new]
                     + [k_page(r) for r in range(pg)] + [v_page(r) for r in range(pg)],
            out_specs=new,
            scratch_shapes=[
                pltpu.VMEM((rows, D_B), BF16),
                pltpu.VMEM((rows, D_B), F32),
                pltpu.VMEM((pg * PAGE_SIZE, D_B), BF16),
                pltpu.VMEM((nks, rows, pg * PAGE_SIZE), F32),
                pltpu.VMEM((nks, rows, pg * PAGE_SIZE), BF16),
                pltpu.VMEM((past // MOBA_BLOCK, B_HEADS, B_HEAD_DIM), F32),
                pltpu.VMEM((rows, D_B), F32),
                pltpu.VMEM((rows, LANES), BF16),
                pltpu.VMEM((rows, LANES), F32),
            ]),
        out_shape=jax.ShapeDtypeStruct(shp, F32),
        compiler_params=_params("arbitrary", "arbitrary"),
        name="moba_sample",
    )(page_table, slope_rows, t_rows, q.reshape(shp), kn.reshape(shp), vn.reshape(shp), zs.reshape(shp),
      *([ck] * pg), *([cv] * pg))
    return out.reshape(n_seq * n_new, D_B)


def _out_proj_kernel(ya_ref, yb_ref, w_ref, x_ref, gt_ref, g_ref, *rest, final):
    wbf_ref = rest[-1]

    @pl.when(pl.program_id(0) == 0)
    def _():
        wbf_ref[...] = w_ref[...].astype(BF16)

    half = ya_ref.shape[1]
    acc = jnp.dot(ya_ref[...].astype(BF16), wbf_ref[0:half, :], preferred_element_type=F32)
    acc = acc + jnp.dot(yb_ref[...].astype(BF16), wbf_ref[half:, :], preferred_element_type=F32)
    x = x_ref[...] + gt_ref[0] * acc
    r = lax.rsqrt(jnp.mean(x * x, axis=-1, keepdims=True) + EPS)
    if final:
        o_ref = rest[0]
        o_ref[...] = (x * r) * g_ref[...]
    else:
        sc_ref, sh_ref, x_out_ref, h_ref = rest[:4]
        x_out_ref[...] = x
        h_ref[...] = ((x * r) * g_ref[...] * (1.0 + sc_ref[0]) + sh_ref[0]).astype(h_ref.dtype)


def _out_proj(ya, yb, ya_col, yb_col, w, x, mod, g, next_mod, tm):
    rows = x.shape[0]
    half = D_MODEL // 2
    final = next_mod is None
    row_f32 = pl.BlockSpec((tm, D_MODEL), lambda i: (i, 0))
    in_specs = [
        pl.BlockSpec((tm, half), lambda i: (i, ya_col)),
        pl.BlockSpec((tm, half), lambda i: (i, yb_col)),
        pl.BlockSpec((D_MODEL, D_MODEL), lambda i: (0, 0), pipeline_mode=pl.Buffered(1)),
        row_f32,
        mod.spec(2, tm),
        pl.BlockSpec((1, D_MODEL), lambda i: (0, 0)),
    ]
    args = [ya, yb, w, x, mod.arr, g.reshape(1, D_MODEL)]
    if final:
        out_specs, out_shape = row_f32, jax.ShapeDtypeStruct((rows, D_MODEL), F32)
    else:
        in_specs += [next_mod.spec(1, tm), next_mod.spec(0, tm)]
        args += [next_mod.arr, next_mod.arr]
        out_specs = [row_f32, row_f32]
        out_shape = [jax.ShapeDtypeStruct((rows, D_MODEL), F32), jax.ShapeDtypeStruct((rows, D_MODEL), BF16)]
    return pl.pallas_call(
        functools.partial(_out_proj_kernel, final=final),
        grid=(rows // tm,),
        in_specs=in_specs,
        out_specs=out_specs,
        out_shape=out_shape,
        scratch_shapes=[pltpu.VMEM((D_MODEL, D_MODEL), BF16)],
        compiler_params=_params("arbitrary"),
        name="out_proj_final" if final else "out_proj",
    )(*args)


def _conv_proj_kernel(h_ref, wa_ref, wb_ref, wz_ref, glu_ref, sz_ref, wbf_ref):
    @pl.when(pl.program_id(1) == 0)
    def _():
        wbf_ref[0] = wa_ref[...].astype(BF16)
        wbf_ref[1] = wb_ref[...].astype(BF16)
        wbf_ref[2] = wz_ref[...].astype(BF16)

    h = h_ref[...]
    a = jnp.dot(h, wbf_ref[0], preferred_element_type=F32)
    b = jnp.dot(h, wbf_ref[1], preferred_element_type=F32)
    glu_ref[...] = a * _sigmoid(b)
    sz_ref[...] = _silu(jnp.dot(h, wbf_ref[2], preferred_element_type=F32))


def _conv_proj(h, w, tm):
    rows = h.shape[0]
    tn = 512
    nj = D_C // tn
    out = pl.BlockSpec((tm, tn), lambda j, i: (i, j))
    return pl.pallas_call(
        _conv_proj_kernel,
        grid=(nj, rows // tm),
        in_specs=[
            pl.BlockSpec((tm, D_MODEL), lambda j, i: (i, 0)),
            pl.BlockSpec((D_MODEL, tn), lambda j, i: (0, j)),
            pl.BlockSpec((D_MODEL, tn), lambda j, i: (0, nj + j)),
            pl.BlockSpec((D_MODEL, tn), lambda j, i: (0, 2 * nj + j)),
        ],
        out_specs=[out, out],
        out_shape=[jax.ShapeDtypeStruct((rows, D_C), F32)] * 2,
        scratch_shapes=[pltpu.VMEM((3, D_MODEL, tn), BF16)],
        compiler_params=_params("arbitrary", "arbitrary"),
        name="conv_proj",
    )(h, w, w, w)


HALO = 32
CONV_LANES = 256


def _conv_ln_kernel(glu_ref, halo_ref, sz_ref, w_ref, cb_ref, g_ref, b_ref, o_ref, buf_scr, y_scr, wb_scr,
                    *, tiles_per_seq, tm):
    first = (pl.program_id(0) % tiles_per_seq) == 0

    @pl.when(pl.program_id(0) == 0)
    def _():
        for k in range(CONV_W):
            wb_scr[k] = jnp.broadcast_to(w_ref[k:k + 1, :], (SUBLANES, D_C))

    @pl.when(first)
    def _():
        buf_scr[0:HALO, :] = jnp.zeros((HALO, D_C), F32)

    @pl.when(jnp.logical_not(first))
    def _():
        buf_scr[0:HALO, :] = halo_ref[...]

    buf_scr[HALO:HALO + tm, :] = glu_ref[...]

    off = HALO - (CONV_W - 1)
    taps = [[(a, SUBLANES * a + r - off) for a in range((CONV_W + off) // SUBLANES + 1)
             if 0 <= SUBLANES * a + r - off < CONV_W] for r in range(SUBLANES)]
    rowid = lax.broadcasted_iota(jnp.int32, (SUBLANES, CONV_LANES), 0)

    for lc in range(D_C // CONV_LANES):
        lanes = slice(lc * CONV_LANES, (lc + 1) * CONV_LANES)

        def z_block(u0, r):
            acc = None
            for a, k in taps[r]:
                term = wb_scr[k, :, lanes] * buf_scr[pl.ds(u0 + SUBLANES * a, SUBLANES), lanes]
                acc = term if acc is None else acc + term
            return acc

        def body(blk, carry):
            t0 = pl.multiple_of(blk * SUBLANES, SUBLANES)
            terms = [z_block(t0, 0)]
            nxt = []
            for r in range(1, SUBLANES):
                zn = z_block(t0 + SUBLANES, r)
                mixed = jnp.where(rowid >= r, carry[r - 1], zn)
                terms.append(pltpu.roll(mixed, SUBLANES - r, 0))
                nxt.append(zn)
            while len(terms) > 1:
                terms = [terms[i] + terms[i + 1] for i in range(0, len(terms), 2)]
            y_scr[pl.ds(t0, SUBLANES), lanes] = terms[0]
            return tuple(nxt)

        init = tuple(z_block(0, r) for r in range(1, SUBLANES))
        lax.fori_loop(0, tm // SUBLANES, body, init, unroll=2)

    y = _layernorm_rows(y_scr[...] + cb_ref[...], g_ref[...], b_ref[...])
    o_ref[...] = (_silu(y) * sz_ref[...]).astype(o_ref.dtype)


def _conv_ln(glu, sz, conv_w, conv_b, ln_g, ln_b, seq, tm):
    rows = glu.shape[0]
    vec = pl.BlockSpec((1, D_C), lambda i: (0, 0))
    tile = pl.BlockSpec((tm, D_C), lambda i: (i, 0))
    return pl.pallas_call(
        functools.partial(_conv_ln_kernel, tiles_per_seq=seq // tm, tm=tm),
        grid=(rows // tm,),
        in_specs=[tile,
                  pl.BlockSpec((HALO, D_C), lambda i: (jnp.maximum(i * (tm // HALO) - 1, 0), 0)),
                  tile,
                  pl.BlockSpec((CONV_W, D_C), lambda i: (0, 0)),
                  vec, vec, vec],
        out_specs=tile,
        out_shape=jax.ShapeDtypeStruct((rows, D_C), BF16),
        scratch_shapes=[pltpu.VMEM((HALO + tm, D_C), F32), pltpu.VMEM((tm, D_C), F32),
                        pltpu.VMEM((CONV_W, SUBLANES, D_C), F32)],
        compiler_params=_params("arbitrary"),
        name="conv_ln",
    )(glu, glu, sz, conv_w, conv_b.reshape(1, D_C), ln_g.reshape(1, D_C), ln_b.reshape(1, D_C))


def _conv_ln_sample_kernel(st_ref, glu_ref, sz_ref, w_ref, cb_ref, g_ref, b_ref, o_ref, ns_ref, buf_scr,
                           *, n_new):
    hist = CONV_W - 1
    buf_scr[0:hist, :] = st_ref[0]
    buf_scr[hist:hist + n_new, :] = glu_ref[0]
    acc = jnp.zeros((n_new, D_C), F32) + cb_ref[...]
    for k in range(CONV_W):
        acc = acc + w_ref[k:k + 1, :] * buf_scr[k:k + n_new, :]
    y = _layernorm_rows(acc, g_ref[...], b_ref[...])
    o_ref[0] = _silu(y) * sz_ref[0]
    ns_ref[0] = buf_scr[n_new:n_new + hist, :]


def _conv_ln_sample(state, glu, sz, conv_w, conv_b, ln_g, ln_b, n_seq, n_new):
    hist = CONV_W - 1
    vec = pl.BlockSpec((1, D_C), lambda s: (0, 0))
    new = pl.BlockSpec((1, n_new, D_C), lambda s: (s, 0, 0))
    st = pl.BlockSpec((1, hist, D_C), lambda s: (s, 0, 0))
    y, ns = pl.pallas_call(
        functools.partial(_conv_ln_sample_kernel, n_new=n_new),
        grid=(n_seq,),
        in_specs=[st, new, new, pl.BlockSpec((CONV_W, D_C), lambda s: (0, 0)), vec, vec, vec],
        out_specs=[new, st],
        out_shape=[jax.ShapeDtypeStruct((n_seq, n_new, D_C), F32),
                   jax.ShapeDtypeStruct((n_seq, hist, D_C), F32)],
        scratch_shapes=[pltpu.VMEM((hist + n_new, D_C), F32)],
        compiler_params=_params("arbitrary"),
        name="conv_ln_sample",
    )(state, glu.reshape(n_seq, n_new, D_C), sz.reshape(n_seq, n_new, D_C), conv_w,
      conv_b.reshape(1, D_C), ln_g.reshape(1, D_C), ln_b.reshape(1, D_C))
    return y.reshape(n_seq * n_new, D_C), ns


def _trunk(x, mods, tm, seq, weights, attend, conv):
    (norm_g, ab_w_in, a_ln_g, a_ln_b, a_ws, a_bs, ab_w_out, c_w_in, c_w_out, final_norm_g) = weights
    tm_row = min(tm, ROW_TILE)
    h = _norm_mod(x, norm_g[0], mods[0], tm_row)
    seg = lambda s, ep: _proj_seg(h, ab_w_in[0], s, ep, a_ln_g[0], a_ln_b[0], tm)
    u, vn, za = seg(0, "gelu"), seg(1, "gelu_ln"), seg(2, "silu")
    q, zb = seg(3, "none"), seg(6, "silu")
    (k, k_heads), (v, v_heads) = seg(4, "heads"), seg(5, "heads")
    ya = _a_mix(u, vn, za, a_ws[0], a_bs[0], seq, tm_row)
    yb = attend(q, k, v, zb)
    x1, h1 = _out_proj(ya, yb, 0, 0, ab_w_out[0], x, mods[0], norm_g[1], mods[1], tm_row)
    glu, sz = _conv_proj(h1, c_w_in[0], tm)
    y2, conv_state = conv(glu, sz)
    y = _out_proj(y2, y2, 0, 1, c_w_out[0], x1, mods[1], final_norm_g, None, tm_row)
    return y, k_heads, v_heads, vn, conv_state


def kernel(x_prompt, x_sample, cache_b_k, cache_b_v, state_c_conv, page_table, c_prompt, c_sample,
           ada_w, ada_b, norm_g, ab_w_in, a_ln_g, a_ln_b, a_ws, a_bs, ab_w_out,
           c_w_in, c_conv_w, c_conv_b, c_ln_g, c_ln_b, c_w_out, final_norm_g):
    batch, seq, _ = x_prompt.shape
    n_seq, n_new, _ = x_sample.shape
    weights = (norm_g, ab_w_in, a_ln_g, a_ln_b, a_ws, a_bs, ab_w_out, c_w_in, c_w_out, final_norm_g)

    mod = _adaln(jnp.concatenate([c_prompt, c_sample], axis=0), ada_w, ada_b)
    mod = mod.reshape(DEPTH, batch + n_seq, 3, D_MODEL)
    tm_p = PROJ_TILE
    mods_p, mods_s = [], []
    for l in range(DEPTH):
        mods_p.append(_Mod(mod[l, :batch].reshape(batch * 3, 1, D_MODEL), seq))
        per_row = jnp.repeat(mod[l, batch:], n_new, axis=0).transpose(1, 0, 2)
        mods_s.append(_Mod(per_row))

    yp, kp, vp, _, glu_p = _trunk(
        x_prompt.reshape(batch * seq, D_MODEL), mods_p, tm_p, seq, weights,
        attend=lambda q, k, v, zb: _moba_prompt(q, k, v, zb, batch, seq),
        conv=lambda glu, sz: (_conv_ln(glu, sz, c_conv_w[0], c_conv_b[0], c_ln_g[0], c_ln_b[0], seq, ROW_TILE), glu))
    ys, ks, vs, vas, conv_s = _trunk(
        x_sample.reshape(n_seq * n_new, D_MODEL), mods_s, n_seq * n_new, n_new, weights,
        attend=lambda q, k, v, zb: _moba_sample(q, k, v, zb, cache_b_k[0], cache_b_v[0], page_table,
                                               n_seq, n_new),
        conv=lambda glu, sz: _conv_ln_sample(state_c_conv[0], glu, sz, c_conv_w[0], c_conv_b[0],
                                             c_ln_g[0], c_ln_b[0], n_seq, n_new))

    hd = (B_HEADS, B_HEAD_DIM)
    conv_p = glu_p.reshape(batch, seq, D_C)[:, seq - (CONV_W - 1):, :]
    return (yp.reshape(batch, seq, D_MODEL), ys.reshape(n_seq, n_new, D_MODEL),
            kp.reshape(1, batch, seq, *hd), vp.reshape(1, batch, seq, *hd),
            ks.reshape(1, n_seq, n_new, *hd), vs.reshape(1, n_seq, n_new, *hd),
            vas.reshape(1, n_seq, n_new, D_A), conv_p[None], conv_s[None])
```

```python
import functools

import numpy as np
import jax
import jax.numpy as jnp
from jax import lax
from jax.experimental import pallas as pl
from jax.experimental.pallas import tpu as pltpu

F32 = jnp.float32
BF16 = jnp.bfloat16
HIGHEST = lax.Precision.HIGHEST

D_MODEL = 2048
DEPTH = 2
PAST_LEN = 8192
PAGE_SIZE = 128
D_A = 1024
A_GROUPS = 8
A_CHUNK = 128
B_HEADS = 8
B_HEAD_DIM = 128
D_B = B_HEADS * B_HEAD_DIM
MOBA_BLOCK = 256
MOBA_TOPK = 3
D_C = D_MODEL
CONV_W = 31
EPS = 1e-6
NEG = -1e30
QK_SCALE = B_HEAD_DIM ** -0.5

VMEM_LIMIT_BYTES = 56 * 1024 * 1024
SUBLANES = 8
LANES = 128
PROJ_TILE = 1024
ROW_TILE = 512

NT_DIMS = (((1,), (1,)), ((), ()))


def _params(*sem):
    return pltpu.CompilerParams(dimension_semantics=sem, vmem_limit_bytes=VMEM_LIMIT_BYTES)


def _div_pow2(x, d):
    shift = int(d).bit_length() - 1
    assert 1 << shift == d
    return lax.shift_right_logical(x, jnp.int32(shift))


def _sigmoid(x):
    return jax.nn.sigmoid(x)


def _silu(x):
    return x * _sigmoid(x)


def _gelu_tanh(x):
    c = np.float32(np.sqrt(2.0 / np.pi))
    return x * (0.5 * (1.0 + jnp.tanh(c * (x + 0.044715 * (x * x * x)))))


def _layernorm_rows(x, g, b):
    mu = jnp.mean(x, axis=-1, keepdims=True)
    xc = x - mu
    var = jnp.mean(xc * xc, axis=-1, keepdims=True)
    return (xc * lax.rsqrt(var + EPS)) * g + b


def _adaln_kernel(c_ref, w_ref, b_ref, o_ref):
    s = _silu(c_ref[...]).astype(BF16)
    o_ref[0] = jnp.dot(s, w_ref[0].astype(BF16), preferred_element_type=F32) + b_ref[0]


def _adaln(c_all, ada_w, ada_b):
    n = c_all.shape[0]
    tn = 1024
    return pl.pallas_call(
        _adaln_kernel,
        grid=(DEPTH, 3 * D_MODEL // tn),
        in_specs=[
            pl.BlockSpec((n, D_MODEL), lambda l, j: (0, 0)),
            pl.BlockSpec((1, D_MODEL, tn), lambda l, j: (l, 0, j)),
            pl.BlockSpec((1, 1, tn), lambda l, j: (l, 0, j)),
        ],
        out_specs=pl.BlockSpec((1, n, tn), lambda l, j: (l, 0, j)),
        out_shape=jax.ShapeDtypeStruct((DEPTH, n, 3 * D_MODEL), F32),
        compiler_params=_params("arbitrary", "arbitrary"),
        name="adaln",
    )(c_all, ada_w, ada_b.reshape(DEPTH, 1, 3 * D_MODEL))


class _Mod:
    def __init__(self, arr, seq_len=None):
        self.arr = arr
        self.rows = arr.shape[1]
        self.seq_len = seq_len

    def spec(self, which, tm):
        if self.seq_len is None:
            return pl.BlockSpec((1, self.rows, D_MODEL), lambda i: (which, 0, 0))
        assert self.seq_len % tm == 0
        tiles = self.seq_len // tm
        return pl.BlockSpec((1, 1, D_MODEL), lambda i: ((i // tiles) * 3 + which, 0, 0))


def _norm_mod_kernel(x_ref, g_ref, sc_ref, sh_ref, h_ref):
    x = x_ref[...]
    r = lax.rsqrt(jnp.mean(x * x, axis=-1, keepdims=True) + EPS)
    h = (x * r) * g_ref[...] * (1.0 + sc_ref[0]) + sh_ref[0]
    h_ref[...] = h.astype(h_ref.dtype)


def _norm_mod(x, g, mod, tm):
    rows = x.shape[0]
    return pl.pallas_call(
        _norm_mod_kernel,
        grid=(rows // tm,),
        in_specs=[
            pl.BlockSpec((tm, D_MODEL), lambda i: (i, 0)),
            pl.BlockSpec((1, D_MODEL), lambda i: (0, 0)),
            mod.spec(1, tm),
            mod.spec(0, tm),
        ],
        out_specs=pl.BlockSpec((tm, D_MODEL), lambda i: (i, 0)),
        out_shape=jax.ShapeDtypeStruct((rows, D_MODEL), BF16),
        compiler_params=_params("arbitrary"),
        name="norm_mod",
    )(x, g.reshape(1, D_MODEL), mod.arr, mod.arr)


def _proj_seg_kernel(h_ref, w_ref, g_ref, b_ref, *rest, epilogue):
    o_ref, wbf_ref = rest[0], rest[-1]

    @pl.when(pl.program_id(0) == 0)
    def _():
        wbf_ref[...] = w_ref[...].astype(BF16)

    acc = jnp.dot(h_ref[...], wbf_ref[...], preferred_element_type=F32)
    if epilogue == "gelu":
        acc = _gelu_tanh(acc)
    elif epilogue == "gelu_ln":
        acc = _layernorm_rows(_gelu_tanh(acc), g_ref[...], b_ref[...])
    elif epilogue == "silu":
        acc = _silu(acc)
    o_ref[...] = acc
    if epilogue == "heads":
        oh_ref = rest[1]
        tm = acc.shape[0]
        for hd in range(B_HEADS):
            oh_ref[pl.ds(hd, tm, stride=B_HEADS), :] = acc[:, hd * B_HEAD_DIM:(hd + 1) * B_HEAD_DIM]


def _proj_seg(h, w, seg, epilogue, ln_g, ln_b, tm):
    rows = h.shape[0]
    tn = 1024
    out_specs = [pl.BlockSpec((tm, tn), lambda i: (i, 0))]
    out_shape = [jax.ShapeDtypeStruct((rows, tn), F32)]
    if epilogue == "heads":
        out_specs.append(pl.BlockSpec((tm * B_HEADS, B_HEAD_DIM), lambda i: (i, 0)))
        out_shape.append(jax.ShapeDtypeStruct((rows * B_HEADS, B_HEAD_DIM), F32))
    out = pl.pallas_call(
        functools.partial(_proj_seg_kernel, epilogue=epilogue),
        grid=(rows // tm,),
        in_specs=[
            pl.BlockSpec((tm, D_MODEL), lambda i: (i, 0)),
            pl.BlockSpec((D_MODEL, tn), lambda i: (0, seg), pipeline_mode=pl.Buffered(1)),
            pl.BlockSpec((1, tn), lambda i: (0, 0)),
            pl.BlockSpec((1, tn), lambda i: (0, 0)),
        ],
        out_specs=out_specs,
        out_shape=out_shape,
        scratch_shapes=[pltpu.VMEM((D_MODEL, tn), BF16)],
        compiler_params=_params("arbitrary"),
        name="proj_" + epilogue,
    )(h, w, ln_g.reshape(1, tn), ln_b.reshape(1, tn))
    return out if epilogue == "heads" else out[0]


def _a_mix_kernel(u_ref, v_ref, z_ref, w_ref, bst_ref, o_ref, *, seq_len, tm):
    row = lax.broadcasted_iota(jnp.int32, (A_CHUNK, A_CHUNK), 0)
    col = lax.broadcasted_iota(jnp.int32, (A_CHUNK, A_CHUNK), 1)
    keep = jnp.where(col <= row, 1.0, 0.0)
    if seq_len < A_CHUNK:
        keep = keep * jnp.where(_div_pow2(row, seq_len) == _div_pow2(col, seq_len), 1.0, 0.0)
    gw = D_A // A_GROUPS
    for g in range(A_GROUPS):
        wg = (w_ref[g] * keep).astype(BF16)
        bias = bst_ref[:, g:g + 1]
        for c in range(tm // A_CHUNK):
            rs = slice(c * A_CHUNK, (c + 1) * A_CHUNK)
            cs = slice(g * gw, (g + 1) * gw)
            mix = jnp.dot(wg, v_ref[rs, cs].astype(BF16), preferred_element_type=F32) + bias
            o_ref[rs, cs] = ((u_ref[rs, cs] * mix) * z_ref[rs, cs]).astype(o_ref.dtype)


def _a_mix(u, vn, z, ws, bs, seq_len, tm):
    rows = u.shape[0]
    n = min(seq_len, A_CHUNK)
    reps = A_CHUNK // n
    wmat = jnp.tile(ws[:, :n, :n], (1, reps, reps))
    bst = jnp.tile(bs[:, :n], (1, reps)).T
    spec = pl.BlockSpec((tm, D_A), lambda i: (i, 0))
    return pl.pallas_call(
        functools.partial(_a_mix_kernel, seq_len=n, tm=tm),
        grid=(rows // tm,),
        in_specs=[spec, spec, spec,
                  pl.BlockSpec((A_GROUPS, A_CHUNK, A_CHUNK), lambda i: (0, 0, 0)),
                  pl.BlockSpec((A_CHUNK, A_GROUPS), lambda i: (0, 0))],
        out_specs=spec,
        out_shape=jax.ShapeDtypeStruct((rows, D_A), BF16),
        compiler_params=_params("arbitrary"),
        name="a_mix",
    )(u, vn, z, wmat, bst)


def _rank_select(g, valid, index, axis, count):
    gm = jnp.where(valid, g, NEG)
    n = g.shape[axis]
    rank = jnp.zeros(g.shape, F32)
    for m in range(n):
        ref = gm[m:m + 1, :] if axis == 0 else gm[:, m:m + 1]
        beats = jnp.where(ref > gm, 1.0, jnp.where(ref == gm, jnp.where(m < index, 1.0, 0.0), 0.0))
        rank = rank + beats
    return jnp.where(valid, jnp.where(rank < float(count), 1.0, 0.0), 0.0), rank


def _moba_prompt_kernel(slope_ref, q_ref, k_ref, v_ref, z_ref, o_ref, kb_scr, vb_scr, s_scr, sel_scr, *, seq):
    nb = seq // MOBA_BLOCK
    slope = slope_ref[pl.program_id(1)]
    k = k_ref[0]
    kb_scr[...] = k.astype(BF16)
    vb_scr[...] = v_ref[0].astype(BF16)

    kbar = jnp.mean(k.reshape(nb, MOBA_BLOCK, B_HEAD_DIM), axis=1)
    kbar = jnp.concatenate([kbar, jnp.zeros((LANES - nb, B_HEAD_DIM), F32)], axis=0)
    qf = q_ref[0] * QK_SCALE
    gt = lax.dot_general(kbar, qf, NT_DIMS, precision=HIGHEST, preferred_element_type=F32)
    blk = lax.broadcasted_iota(jnp.int32, (nb, seq), 0)
    own = _div_pow2(lax.broadcasted_iota(jnp.int32, (nb, seq), 1), MOBA_BLOCK)
    sel, _ = _rank_select(gt[0:nb, :], blk < own, blk, 0, MOBA_TOPK)
    sel = jnp.concatenate([sel, jnp.zeros((LANES - nb, seq), F32)], axis=0)
    sel_scr[...] = sel.T

    rc = (lax.broadcasted_iota(jnp.int32, (MOBA_BLOCK, MOBA_BLOCK), 0)
          - lax.broadcasted_iota(jnp.int32, (MOBA_BLOCK, MOBA_BLOCK), 1)).astype(F32)
    for jb in range(nb):
        rows = slice(jb * MOBA_BLOCK, (jb + 1) * MOBA_BLOCK)
        nk = (jb + 1) * MOBA_BLOCK
        qb = (q_ref[0, rows, :] * QK_SCALE).astype(BF16)
        s = lax.dot_general(qb, kb_scr[0:nk, :], NT_DIMS, preferred_element_type=F32)
        m = None
        for n in range(jb + 1):
            cols = slice(n * MOBA_BLOCK, (n + 1) * MOBA_BLOCK)
            dist = rc + float(MOBA_BLOCK * (jb - n))
            slab = s[:, cols] - slope * dist
            if n < jb:
                slab = jnp.where(sel_scr[rows, n:n + 1] > 0.5, slab, NEG)
            else:
                slab = jnp.where(rc >= 0.0, slab, NEG)
            s_scr[:, cols] = slab
            mx = jnp.max(slab, axis=-1, keepdims=True)
            m = mx if m is None else jnp.maximum(m, mx)
        p = jnp.exp(s_scr[:, 0:nk] - m)
        l = jnp.sum(p, axis=-1, keepdims=True)
        o = jnp.dot(p.astype(BF16), vb_scr[0:nk, :], preferred_element_type=F32)
        o_ref[0, rows, :] = ((o / l) * z_ref[0, rows, :]).astype(o_ref.dtype)


def _alibi_slopes():
    return jnp.asarray(np.array([2.0 ** (-8.0 * (h + 1) / B_HEADS) for h in range(B_HEADS)], np.float32))


def _moba_prompt(q, k, v, zs, batch, seq):
    shp = (batch, seq, D_B)
    spec = pl.BlockSpec((1, seq, B_HEAD_DIM), lambda b, h: (b, 0, h))
    out = pl.pallas_call(
        functools.partial(_moba_prompt_kernel, seq=seq),
        grid=(batch, B_HEADS),
        in_specs=[pl.BlockSpec(memory_space=pltpu.SMEM), spec, spec, spec, spec],
        out_specs=spec,
        out_shape=jax.ShapeDtypeStruct(shp, BF16),
        scratch_shapes=[pltpu.VMEM((seq, B_HEAD_DIM), BF16), pltpu.VMEM((seq, B_HEAD_DIM), BF16),
                        pltpu.VMEM((MOBA_BLOCK, seq), F32), pltpu.VMEM((seq, LANES), F32)],
        compiler_params=_params("arbitrary", "arbitrary"),
        name="moba_prompt",
    )(_alibi_slopes(), q.reshape(shp), k.reshape(shp), v.reshape(shp), zs.reshape(shp))
    return out.reshape(batch * seq, D_B)


CHUNK_PAGES = 8
K_RING = 3


def _moba_sample_kernel(pt_ref, slope_ref, trow_ref, q_ref, kn_ref, vn_ref, z_ref, ck_ref, cv_ref, o_ref,
                        kbuf, vbuf, ksem, vsem, isem, idx_v, idx_s,
                        qbd_scr, qbf_scr, cat_scr, s_scr, p_scr, ksum_scr, onew_scr, l_scr,
                        *, n_seq, n_new, n_pages):
    pg = CHUNK_PAGES
    nch = n_pages // pg
    total = n_seq * nch
    n_blocks = n_pages * PAGE_SIZE // MOBA_BLOCK
    blocks_per_chunk = pg * PAGE_SIZE // MOBA_BLOCK
    pages_per_block = MOBA_BLOCK // PAGE_SIZE
    rows = n_new * B_HEADS
    t = pl.program_id(0)
    head_of_lane = _div_pow2(lax.broadcasted_iota(jnp.int32, (B_HEADS, D_B), 1), B_HEAD_DIM)
    diag = head_of_lane == lax.broadcasted_iota(jnp.int32, (B_HEADS, D_B), 0)
    slope = slope_ref[:, 0:1]
    trow = trow_ref[:, 0:1]

    def k_copy(page, slot, r):
        return pltpu.make_async_copy(ck_ref.at[page], kbuf.at[slot, r], ksem.at[slot])

    def start_chunk(seq, ch, slot):
        for r in range(pg):
            k_copy(pt_ref[seq, ch * pg + r], slot, r).start()

    def v_copy(page, r, j, half):
        dst = vbuf.at[r, pl.ds((j * pages_per_block + half) * PAGE_SIZE, PAGE_SIZE), :]
        return pltpu.make_async_copy(cv_ref.at[page, :, r % B_HEADS, :], dst, vsem.at[0])

    @pl.when(t == 0)
    def _():
        for f0 in range(K_RING - 1):
            start_chunk(0, f0, f0)

    @pl.when(t < n_seq)
    def _():
        tiles = []
        for qi in range(n_new):
            qrow = q_ref[0, qi:qi + 1, :] * QK_SCALE
            tiles.append(jnp.where(diag, jnp.broadcast_to(qrow, (B_HEADS, D_B)), 0.0))
        qbd = jnp.concatenate(tiles, axis=0)
        qbf_scr[...] = qbd
        qbd_scr[...] = qbd.astype(BF16)

        def chunk_body(c, carry):
            f = t * nch + c
            ahead = f + (K_RING - 1)

            @pl.when(ahead < total)
            def _():
                start_chunk(lax.div(ahead, jnp.int32(nch)), lax.rem(ahead, jnp.int32(nch)),
                            lax.rem(ahead, jnp.int32(K_RING)))

            slot = lax.rem(f, jnp.int32(K_RING))
            for r in range(pg):
                k_copy(0, slot, r).wait()
            pair = None
            for r in range(pg):
                for hd in range(B_HEADS):
                    part = kbuf[slot, r, pl.ds(hd, PAGE_SIZE, stride=B_HEADS), :]
                    cat_scr[r * PAGE_SIZE:(r + 1) * PAGE_SIZE,
                            hd * B_HEAD_DIM:(hd + 1) * B_HEAD_DIM] = part.astype(BF16)
                ps = jnp.sum(kbuf[slot, r].reshape(PAGE_SIZE, B_HEADS, B_HEAD_DIM), axis=0)
                if r % pages_per_block == 0:
                    pair = ps
                else:
                    pair = pair + ps
                if r % pages_per_block == pages_per_block - 1:
                    ksum_scr[c * blocks_per_chunk + r // pages_per_block] = pair
            s_scr[c] = lax.dot_general(qbd_scr[...], cat_scr[...], NT_DIMS, preferred_element_type=F32)
            return carry

        lax.fori_loop(0, nch, chunk_body, 0)

    @pl.when(t >= 1)
    def _():
        for r in range(rows):
            for j in range(MOBA_TOPK):
                for half in range(pages_per_block):
                    v_copy(0, r, j, half).wait()
        head_rows = []
        for r in range(rows):
            parts = [p_scr[idx_s[r, j], r:r + 1, :] for j in range(MOBA_TOPK)]
            p_r = jnp.broadcast_to(jnp.concatenate(parts, axis=1), (SUBLANES, MOBA_TOPK * MOBA_BLOCK))
            o = jnp.dot(p_r.astype(BF16), vbuf[r].astype(BF16), preferred_element_type=F32)
            head_rows.append(o[0:1, :])
        out_rows = []
        for qi in range(n_new):
            sl = slice(qi * B_HEADS, (qi + 1) * B_HEADS)
            o_sel = jnp.concatenate(head_rows[sl], axis=1)
            o_new = jnp.sum(jnp.where(diag, onew_scr[sl, :], 0.0), axis=0, keepdims=True)
            l_row = jnp.concatenate([l_scr[r:r + 1, :] for r in range(sl.start, sl.stop)], axis=1)
            out_rows.append((o_sel + o_new) / l_row)
        o_ref[0] = jnp.concatenate(out_rows, axis=0) * z_ref[0]

    @pl.when(t < n_seq)
    def _():
        kbar = jnp.concatenate([ksum_scr[:, hd, :] for hd in range(B_HEADS)], axis=1)
        kbar = kbar * (1.0 / MOBA_BLOCK)
        g = lax.dot_general(qbf_scr[...], kbar, NT_DIMS, precision=HIGHEST,
                            preferred_element_type=F32)
        blk = lax.broadcasted_iota(jnp.int32, (rows, n_blocks), 1)
        sel, rank = _rank_select(g, blk >= 0, blk, 1, MOBA_TOPK)

        lane = lax.broadcasted_iota(jnp.int32, (rows, LANES), 1)
        ids = jnp.zeros((rows, LANES), F32)
        for j in range(MOBA_TOPK):
            col = jnp.sum(jnp.where(rank == float(j), blk.astype(F32), 0.0), axis=-1, keepdims=True)
            ids = jnp.where(lane == j, col, ids)
        idx_v[...] = ids.astype(jnp.int32)
        to_smem = pltpu.make_async_copy(idx_v, idx_s, isem.at[0])
        to_smem.start()
        to_smem.wait()
        for r in range(rows):
            for j in range(MOBA_TOPK):
                for half in range(pages_per_block):
                    v_copy(pt_ref[t, idx_s[r, j] * pages_per_block + half], r, j, half).start()

        key = lax.broadcasted_iota(jnp.int32, (rows, MOBA_BLOCK), 1).astype(F32)
        m = None
        for n in range(n_blocks):
            gi, a = divmod(n, blocks_per_chunk)
            dist = trow - (float(n * MOBA_BLOCK) + key)
            slab = s_scr[gi, :, a * MOBA_BLOCK:(a + 1) * MOBA_BLOCK] - slope * dist
            slab = jnp.where(sel[:, n:n + 1] > 0.5, slab, NEG)
            p_scr[n] = slab
            mx = jnp.max(slab, axis=-1, keepdims=True)
            m = mx if m is None else jnp.maximum(m, mx)
        knew = jnp.concatenate([kn_ref[0], jnp.zeros((LANES - n_new, D_B), F32)], axis=0).astype(BF16)
        s_new = lax.dot_general(qbd_scr[...], knew, NT_DIMS, preferred_element_type=F32)
        dist_new = trow - (float(n_pages * PAGE_SIZE) + lane.astype(F32))
        s_new = jnp.where(dist_new >= 0.0, s_new - slope * dist_new, NEG)
        m = jnp.maximum(m, jnp.max(s_new, axis=-1, keepdims=True))
        l = jnp.zeros((rows, 1), F32)
        for n in range(n_blocks):
            p = jnp.exp(p_scr[n] - m)
            l = l + jnp.sum(p, axis=-1, keepdims=True)
            p_scr[n] = p
        pn = jnp.exp(s_new - m)
        l = l + jnp.sum(pn, axis=-1, keepdims=True)
        l_scr[...] = jnp.broadcast_to(l, (rows, LANES))
        vnew = jnp.concatenate([vn_ref[0], jnp.zeros((LANES - n_new, D_B), F32)], axis=0).astype(BF16)
        onew_scr[...] = jnp.dot(pn.astype(BF16), vnew, preferred_element_type=F32)


def _moba_sample(q, kn, vn, zs, cache_k, cache_v, page_table, n_seq, n_new):
    n_pages = page_table.shape[1]
    n_phys = cache_k.shape[0]
    pg = CHUNK_PAGES
    nch = n_pages // pg
    assert n_pages % pg == 0 and K_RING - 1 <= nch
    rows = n_new * B_HEADS
    past = n_pages * PAGE_SIZE
    shp = (n_seq, n_new, D_B)
    slope_rows = np.tile(np.array([2.0 ** (-8.0 * (h + 1) / B_HEADS) for h in range(B_HEADS)], np.float32),
                         n_new)
    t_rows = np.repeat(past + np.arange(n_new, dtype=np.float32), B_HEADS)
    slope_rows = jnp.asarray(np.tile(slope_rows[:, None], (1, LANES)))
    t_rows = jnp.asarray(np.tile(t_rows[:, None], (1, LANES)))

    const = pl.BlockSpec((rows, LANES), lambda t, pt: (0, 0))
    cur = pl.BlockSpec((1, n_new, D_B), lambda t, pt: (jnp.minimum(t, n_seq - 1), 0, 0))
    prev = pl.BlockSpec((1, n_new, D_B), lambda t, pt: (jnp.maximum(t - 1, 0), 0, 0))
    hbm = pl.BlockSpec(memory_space=pl.ANY)

    page_rows = PAGE_SIZE * B_HEADS
    n_blocks = past // MOBA_BLOCK
    out = pl.pallas_call(
        functools.partial(_moba_sample_kernel, n_seq=n_seq, n_new=n_new, n_pages=n_pages),
        grid_spec=pltpu.PrefetchScalarGridSpec(
            num_scalar_prefetch=1,
            grid=(n_seq + 1,),
            in_specs=[const, const, cur, cur, cur, prev, hbm, hbm],
            out_specs=prev,
            scratch_shapes=[
                pltpu.VMEM((K_RING, pg, page_rows, B_HEAD_DIM), F32),
                pltpu.VMEM((rows, MOBA_TOPK * MOBA_BLOCK, B_HEAD_DIM), F32),
                pltpu.SemaphoreType.DMA((K_RING,)),
                pltpu.SemaphoreType.DMA((1,)),
                pltpu.SemaphoreType.DMA((1,)),
                pltpu.VMEM((rows, LANES), jnp.int32),
                pltpu.SMEM((rows, LANES), jnp.int32),
                pltpu.VMEM((rows, D_B), BF16),
                pltpu.VMEM((rows, D_B), F32),
                pltpu.VMEM((pg * PAGE_SIZE, D_B), BF16),
                pltpu.VMEM((nch, rows, pg * PAGE_SIZE), F32),
                pltpu.VMEM((n_blocks, rows, MOBA_BLOCK), F32),
                pltpu.VMEM((n_blocks, B_HEADS, B_HEAD_DIM), F32),
                pltpu.VMEM((rows, D_B), F32),
                pltpu.VMEM((rows, LANES), F32),
            ]),
        out_shape=jax.ShapeDtypeStruct(shp, F32),
        compiler_params=_params("arbitrary"),
        name="moba_sample",
    )(page_table, slope_rows, t_rows, q.reshape(shp), kn.reshape(shp), vn.reshape(shp), zs.reshape(shp),
      cache_k.reshape(n_phys, page_rows, B_HEAD_DIM), cache_v)
    return out.reshape(n_seq * n_new, D_B)


def _out_proj_kernel(ya_ref, yb_ref, w_ref, x_ref, gt_ref, g_ref, *rest, final):
    wbf_ref = rest[-1]

    @pl.when(pl.program_id(0) == 0)
    def _():
        wbf_ref[...] = w_ref[...].astype(BF16)

    half = ya_ref.shape[1]
    acc = jnp.dot(ya_ref[...].astype(BF16), wbf_ref[0:half, :], preferred_element_type=F32)
    acc = acc + jnp.dot(yb_ref[...].astype(BF16), wbf_ref[half:, :], preferred_element_type=F32)
    x = x_ref[...] + gt_ref[0] * acc
    r = lax.rsqrt(jnp.mean(x * x, axis=-1, keepdims=True) + EPS)
    if final:
        o_ref = rest[0]
        o_ref[...] = (x * r) * g_ref[...]
    else:
        sc_ref, sh_ref, x_out_ref, h_ref = rest[:4]
        x_out_ref[...] = x
        h_ref[...] = ((x * r) * g_ref[...] * (1.0 + sc_ref[0]) + sh_ref[0]).astype(h_ref.dtype)


def _out_proj(ya, yb, ya_col, yb_col, w, x, mod, g, next_mod, tm):
    rows = x.shape[0]
    half = D_MODEL // 2
    final = next_mod is None
    row_f32 = pl.BlockSpec((tm, D_MODEL), lambda i: (i, 0))
    in_specs = [
        pl.BlockSpec((tm, half), lambda i: (i, ya_col)),
        pl.BlockSpec((tm, half), lambda i: (i, yb_col)),
        pl.BlockSpec((D_MODEL, D_MODEL), lambda i: (0, 0), pipeline_mode=pl.Buffered(1)),
        row_f32,
        mod.spec(2, tm),
        pl.BlockSpec((1, D_MODEL), lambda i: (0, 0)),
    ]
    args = [ya, yb, w, x, mod.arr, g.reshape(1, D_MODEL)]
    if final:
        out_specs, out_shape = row_f32, jax.ShapeDtypeStruct((rows, D_MODEL), F32)
    else:
        in_specs += [next_mod.spec(1, tm), next_mod.spec(0, tm)]
        args += [next_mod.arr, next_mod.arr]
        out_specs = [row_f32, row_f32]
        out_shape = [jax.ShapeDtypeStruct((rows, D_MODEL), F32), jax.ShapeDtypeStruct((rows, D_MODEL), BF16)]
    return pl.pallas_call(
        functools.partial(_out_proj_kernel, final=final),
        grid=(rows // tm,),
        in_specs=in_specs,
        out_specs=out_specs,
        out_shape=out_shape,
        scratch_shapes=[pltpu.VMEM((D_MODEL, D_MODEL), BF16)],
        compiler_params=_params("arbitrary"),
        name="out_proj_final" if final else "out_proj",
    )(*args)


def _conv_proj_kernel(h_ref, wa_ref, wb_ref, wz_ref, glu_ref, sz_ref, wbf_ref):
    @pl.when(pl.program_id(1) == 0)
    def _():
        wbf_ref[0] = wa_ref[...].astype(BF16)
        wbf_ref[1] = wb_ref[...].astype(BF16)
        wbf_ref[2] = wz_ref[...].astype(BF16)

    h = h_ref[...]
    a = jnp.dot(h, wbf_ref[0], preferred_element_type=F32)
    b = jnp.dot(h, wbf_ref[1], preferred_element_type=F32)
    glu_ref[...] = a * _sigmoid(b)
    sz_ref[...] = _silu(jnp.dot(h, wbf_ref[2], preferred_element_type=F32))


def _conv_proj(h, w, tm):
    rows = h.shape[0]
    tn = 512
    nj = D_C // tn
    out = pl.BlockSpec((tm, tn), lambda j, i: (i, j))
    return pl.pallas_call(
        _conv_proj_kernel,
        grid=(nj, rows // tm),
        in_specs=[
            pl.BlockSpec((tm, D_MODEL), lambda j, i: (i, 0)),
            pl.BlockSpec((D_MODEL, tn), lambda j, i: (0, j)),
            pl.BlockSpec((D_MODEL, tn), lambda j, i: (0, nj + j)),
            pl.BlockSpec((D_MODEL, tn), lambda j, i: (0, 2 * nj + j)),
        ],
        out_specs=[out, out],
        out_shape=[jax.ShapeDtypeStruct((rows, D_C), F32)] * 2,
        scratch_shapes=[pltpu.VMEM((3, D_MODEL, tn), BF16)],
        compiler_params=_params("arbitrary", "arbitrary"),
        name="conv_proj",
    )(h, w, w, w)


HALO = 32
CONV_LANES = 256


def _conv_ln_kernel(glu_ref, halo_ref, sz_ref, w_ref, cb_ref, g_ref, b_ref, o_ref, buf_scr, y_scr, wb_scr,
                    *, tiles_per_seq, tm):
    first = (pl.program_id(0) % tiles_per_seq) == 0

    @pl.when(pl.program_id(0) == 0)
    def _():
        for k in range(CONV_W):
            wb_scr[k] = jnp.broadcast_to(w_ref[k:k + 1, :], (SUBLANES, D_C))

    @pl.when(first)
    def _():
        buf_scr[0:HALO, :] = jnp.zeros((HALO, D_C), F32)

    @pl.when(jnp.logical_not(first))
    def _():
        buf_scr[0:HALO, :] = halo_ref[...]

    buf_scr[HALO:HALO + tm, :] = glu_ref[...]

    off = HALO - (CONV_W - 1)
    taps = [[(a, SUBLANES * a + r - off) for a in range((CONV_W + off) // SUBLANES + 1)
             if 0 <= SUBLANES * a + r - off < CONV_W] for r in range(SUBLANES)]
    rowid = lax.broadcasted_iota(jnp.int32, (SUBLANES, CONV_LANES), 0)

    for lc in range(D_C // CONV_LANES):
        lanes = slice(lc * CONV_LANES, (lc + 1) * CONV_LANES)

        def z_block(u0, r):
            acc = None
            for a, k in taps[r]:
                term = wb_scr[k, :, lanes] * buf_scr[pl.ds(u0 + SUBLANES * a, SUBLANES), lanes]
                acc = term if acc is None else acc + term
            return acc

        def body(blk, carry):
            t0 = pl.multiple_of(blk * SUBLANES, SUBLANES)
            terms = [z_block(t0, 0)]
            nxt = []
            for r in range(1, SUBLANES):
                zn = z_block(t0 + SUBLANES, r)
                mixed = jnp.where(rowid >= r, carry[r - 1], zn)
                terms.append(pltpu.roll(mixed, SUBLANES - r, 0))
                nxt.append(zn)
            while len(terms) > 1:
                terms = [terms[i] + terms[i + 1] for i in range(0, len(terms), 2)]
            y_scr[pl.ds(t0, SUBLANES), lanes] = terms[0]
            return tuple(nxt)

        init = tuple(z_block(0, r) for r in range(1, SUBLANES))
        lax.fori_loop(0, tm // SUBLANES, body, init, unroll=2)

    y = _layernorm_rows(y_scr[...] + cb_ref[...], g_ref[...], b_ref[...])
    o_ref[...] = (_silu(y) * sz_ref[...]).astype(o_ref.dtype)


def _conv_ln(glu, sz, conv_w, conv_b, ln_g, ln_b, seq, tm):
    rows = glu.shape[0]
    vec = pl.BlockSpec((1, D_C), lambda i: (0, 0))
    tile = pl.BlockSpec((tm, D_C), lambda i: (i, 0))
    return pl.pallas_call(
        functools.partial(_conv_ln_kernel, tiles_per_seq=seq // tm, tm=tm),
        grid=(rows // tm,),
        in_specs=[tile,
                  pl.BlockSpec((HALO, D_C), lambda i: (jnp.maximum(i * (tm // HALO) - 1, 0), 0)),
                  tile,
                  pl.BlockSpec((CONV_W, D_C), lambda i: (0, 0)),
                  vec, vec, vec],
        out_specs=tile,
        out_shape=jax.ShapeDtypeStruct((rows, D_C), BF16),
        scratch_shapes=[pltpu.VMEM((HALO + tm, D_C), F32), pltpu.VMEM((tm, D_C), F32),
                        pltpu.VMEM((CONV_W, SUBLANES, D_C), F32)],
        compiler_params=_params("arbitrary"),
        name="conv_ln",
    )(glu, glu, sz, conv_w, conv_b.reshape(1, D_C), ln_g.reshape(1, D_C), ln_b.reshape(1, D_C))


def _conv_ln_sample_kernel(st_ref, glu_ref, sz_ref, w_ref, cb_ref, g_ref, b_ref, o_ref, ns_ref, buf_scr,
                           *, n_new):
    hist = CONV_W - 1
    buf_scr[0:hist, :] = st_ref[0]
    buf_scr[hist:hist + n_new, :] = glu_ref[0]
    acc = jnp.zeros((n_new, D_C), F32) + cb_ref[...]
    for k in range(CONV_W):
        acc = acc + w_ref[k:k + 1, :] * buf_scr[k:k + n_new, :]
    y = _layernorm_rows(acc, g_ref[...], b_ref[...])
    o_ref[0] = _silu(y) * sz_ref[0]
    ns_ref[0] = buf_scr[n_new:n_new + hist, :]


def _conv_ln_sample(state, glu, sz, conv_w, conv_b, ln_g, ln_b, n_seq, n_new):
    hist = CONV_W - 1
    vec = pl.BlockSpec((1, D_C), lambda s: (0, 0))
    new = pl.BlockSpec((1, n_new, D_C), lambda s: (s, 0, 0))
    st = pl.BlockSpec((1, hist, D_C), lambda s: (s, 0, 0))
    y, ns = pl.pallas_call(
        functools.partial(_conv_ln_sample_kernel, n_new=n_new),
        grid=(n_seq,),
        in_specs=[st, new, new, pl.BlockSpec((CONV_W, D_C), lambda s: (0, 0)), vec, vec, vec],
        out_specs=[new, st],
        out_shape=[jax.ShapeDtypeStruct((n_seq, n_new, D_C), F32),
                   jax.ShapeDtypeStruct((n_seq, hist, D_C), F32)],
        scratch_shapes=[pltpu.VMEM((hist + n_new, D_C), F32)],
        compiler_params=_params("arbitrary"),
        name="conv_ln_sample",
    )(state, glu.reshape(n_seq, n_new, D_C), sz.reshape(n_seq, n_new, D_C), conv_w,
      conv_b.reshape(1, D_C), ln_g.reshape(1, D_C), ln_b.reshape(1, D_C))
    return y.reshape(n_seq * n_new, D_C), ns


def _trunk(x, mods, tm, seq, weights, attend, conv):
    (norm_g, ab_w_in, a_ln_g, a_ln_b, a_ws, a_bs, ab_w_out, c_w_in, c_w_out, final_norm_g) = weights
    tm_row = min(tm, ROW_TILE)
    h = _norm_mod(x, norm_g[0], mods[0], tm_row)
    seg = lambda s, ep: _proj_seg(h, ab_w_in[0], s, ep, a_ln_g[0], a_ln_b[0], tm)
    u, vn, za = seg(0, "gelu"), seg(1, "gelu_ln"), seg(2, "silu")
    q, zb = seg(3, "none"), seg(6, "silu")
    (k, k_heads), (v, v_heads) = seg(4, "heads"), seg(5, "heads")
    ya = _a_mix(u, vn, za, a_ws[0], a_bs[0], seq, tm_row)
    yb = attend(q, k, v, zb)
    x1, h1 = _out_proj(ya, yb, 0, 0, ab_w_out[0], x, mods[0], norm_g[1], mods[1], tm_row)
    glu, sz = _conv_proj(h1, c_w_in[0], tm)
    y2, conv_state = conv(glu, sz)
    y = _out_proj(y2, y2, 0, 1, c_w_out[0], x1, mods[1], final_norm_g, None, tm_row)
    return y, k_heads, v_heads, vn, conv_state


def kernel(x_prompt, x_sample, cache_b_k, cache_b_v, state_c_conv, page_table, c_prompt, c_sample,
           ada_w, ada_b, norm_g, ab_w_in, a_ln_g, a_ln_b, a_ws, a_bs, ab_w_out,
           c_w_in, c_conv_w, c_conv_b, c_ln_g, c_ln_b, c_w_out, final_norm_g):
    batch, seq, _ = x_prompt.shape
    n_seq, n_new, _ = x_sample.shape
    weights = (norm_g, ab_w_in, a_ln_g, a_ln_b, a_ws, a_bs, ab_w_out, c_w_in, c_w_out, final_norm_g)

    mod = _adaln(jnp.concatenate([c_prompt, c_sample], axis=0), ada_w, ada_b)
    mod = mod.reshape(DEPTH, batch + n_seq, 3, D_MODEL)
    tm_p = PROJ_TILE
    mods_p, mods_s = [], []
    for l in range(DEPTH):
        mods_p.append(_Mod(mod[l, :batch].reshape(batch * 3, 1, D_MODEL), seq))
        per_row = jnp.repeat(mod[l, batch:], n_new, axis=0).transpose(1, 0, 2)
        mods_s.append(_Mod(per_row))

    yp, kp, vp, _, glu_p = _trunk(
        x_prompt.reshape(batch * seq, D_MODEL), mods_p, tm_p, seq, weights,
        attend=lambda q, k, v, zb: _moba_prompt(q, k, v, zb, batch, seq),
        conv=lambda glu, sz: (_conv_ln(glu, sz, c_conv_w[0], c_conv_b[0], c_ln_g[0], c_ln_b[0], seq, ROW_TILE), glu))
    ys, ks, vs, vas, conv_s = _trunk(
        x_sample.reshape(n_seq * n_new, D_MODEL), mods_s, n_seq * n_new, n_new, weights,
        attend=lambda q, k, v, zb: _moba_sample(q, k, v, zb, cache_b_k[0], cache_b_v[0], page_table,
                                               n_seq, n_new),
        conv=lambda glu, sz: _conv_ln_sample(state_c_conv[0], glu, sz, c_conv_w[0], c_conv_b[0],
                                             c_ln_g[0], c_ln_b[0], n_seq, n_new))

    hd = (B_HEADS, B_HEAD_DIM)
    conv_p = glu_p.reshape(batch, seq, D_C)[:, seq - (CONV_W - 1):, :]
    return (yp.reshape(batch, seq, D_MODEL), ys.reshape(n_seq, n_new, D_MODEL),
            kp.reshape(1, batch, seq, *hd), vp.reshape(1, batch, seq, *hd),
            ks.reshape(1, n_seq, n_new, *hd), vs.reshape(1, n_seq, n_new, *hd),
            vas.reshape(1, n_seq, n_new, D_A), conv_p[None], conv_s[None])
```

```python
import functools

import numpy as np
import jax
import jax.numpy as jnp
from jax import lax
from jax.experimental import pallas as pl
from jax.experimental.pallas import tpu as pltpu

F32 = jnp.float32
BF16 = jnp.bfloat16
HIGHEST = lax.Precision.HIGHEST

D_MODEL = 2048
DEPTH = 2
PAST_LEN = 8192
PAGE_SIZE = 128
D_A = 1024
A_GROUPS = 8
A_CHUNK = 128
B_HEADS = 8
B_HEAD_DIM = 128
D_B = B_HEADS * B_HEAD_DIM
MOBA_BLOCK = 256
MOBA_TOPK = 3
D_C = D_MODEL
CONV_W = 31
EPS = 1e-6
NEG = -1e30
QK_SCALE = B_HEAD_DIM ** -0.5

VMEM_LIMIT_BYTES = 56 * 1024 * 1024
SUBLANES = 8
LANES = 128
PROJ_TILE = 1024
ROW_TILE = 512

NT_DIMS = (((1,), (1,)), ((), ()))


def _params(*sem):
    return pltpu.CompilerParams(dimension_semantics=sem, vmem_limit_bytes=VMEM_LIMIT_BYTES)


def _div_pow2(x, d):
    shift = int(d).bit_length() - 1
    assert 1 << shift == d
    return lax.shift_right_logical(x, jnp.int32(shift))


def _sigmoid(x):
    return jax.nn.sigmoid(x)


def _silu(x):
    return x * _sigmoid(x)


def _gelu_tanh(x):
    c = np.float32(np.sqrt(2.0 / np.pi))
    return x * (0.5 * (1.0 + jnp.tanh(c * (x + 0.044715 * (x * x * x)))))


def _layernorm_rows(x, g, b):
    mu = jnp.mean(x, axis=-1, keepdims=True)
    xc = x - mu
    var = jnp.mean(xc * xc, axis=-1, keepdims=True)
    return (xc * lax.rsqrt(var + EPS)) * g + b


def _adaln_kernel(c_ref, w_ref, b_ref, o_ref):
    s = _silu(c_ref[...]).astype(BF16)
    o_ref[0] = jnp.dot(s, w_ref[0].astype(BF16), preferred_element_type=F32) + b_ref[0]


def _adaln(c_all, ada_w, ada_b):
    n = c_all.shape[0]
    tn = 1024
    return pl.pallas_call(
        _adaln_kernel,
        grid=(DEPTH, 3 * D_MODEL // tn),
        in_specs=[
            pl.BlockSpec((n, D_MODEL), lambda l, j: (0, 0)),
            pl.BlockSpec((1, D_MODEL, tn), lambda l, j: (l, 0, j)),
            pl.BlockSpec((1, 1, tn), lambda l, j: (l, 0, j)),
        ],
        out_specs=pl.BlockSpec((1, n, tn), lambda l, j: (l, 0, j)),
        out_shape=jax.ShapeDtypeStruct((DEPTH, n, 3 * D_MODEL), F32),
        compiler_params=_params("arbitrary", "arbitrary"),
        name="adaln",
    )(c_all, ada_w, ada_b.reshape(DEPTH, 1, 3 * D_MODEL))


class _Mod:
    def __init__(self, arr, seq_len=None):
        self.arr = arr
        self.rows = arr.shape[1]
        self.seq_len = seq_len

    def spec(self, which, tm):
        if self.seq_len is None:
            return pl.BlockSpec((1, self.rows, D_MODEL), lambda i: (which, 0, 0))
        assert self.seq_len % tm == 0
        tiles = self.seq_len // tm
        return pl.BlockSpec((1, 1, D_MODEL), lambda i: ((i // tiles) * 3 + which, 0, 0))


def _norm_mod_kernel(x_ref, g_ref, sc_ref, sh_ref, h_ref):
    x = x_ref[...]
    r = lax.rsqrt(jnp.mean(x * x, axis=-1, keepdims=True) + EPS)
    h = (x * r) * g_ref[...] * (1.0 + sc_ref[0]) + sh_ref[0]
    h_ref[...] = h.astype(h_ref.dtype)


def _norm_mod(x, g, mod, tm):
    rows = x.shape[0]
    return pl.pallas_call(
        _norm_mod_kernel,
        grid=(rows // tm,),
        in_specs=[
            pl.BlockSpec((tm, D_MODEL), lambda i: (i, 0)),
            pl.BlockSpec((1, D_MODEL), lambda i: (0, 0)),
            mod.spec(1, tm),
            mod.spec(0, tm),
        ],
        out_specs=pl.BlockSpec((tm, D_MODEL), lambda i: (i, 0)),
        out_shape=jax.ShapeDtypeStruct((rows, D_MODEL), BF16),
        compiler_params=_params("arbitrary"),
        name="norm_mod",
    )(x, g.reshape(1, D_MODEL), mod.arr, mod.arr)


def _proj_seg_kernel(h_ref, w_ref, g_ref, b_ref, *rest, epilogue):
    o_ref, wbf_ref = rest[0], rest[-1]

    @pl.when(pl.program_id(0) == 0)
    def _():
        wbf_ref[...] = w_ref[...].astype(BF16)

    acc = jnp.dot(h_ref[...], wbf_ref[...], preferred_element_type=F32)
    if epilogue == "gelu":
        acc = _gelu_tanh(acc)
    elif epilogue == "gelu_ln":
        acc = _layernorm_rows(_gelu_tanh(acc), g_ref[...], b_ref[...])
    elif epilogue == "silu":
        acc = _silu(acc)
    o_ref[...] = acc
    if epilogue == "heads":
        oh_ref = rest[1]
        tm = acc.shape[0]
        for hd in range(B_HEADS):
            oh_ref[pl.ds(hd, tm, stride=B_HEADS), :] = acc[:, hd * B_HEAD_DIM:(hd + 1) * B_HEAD_DIM]


def _proj_seg(h, w, seg, epilogue, ln_g, ln_b, tm):
    rows = h.shape[0]
    tn = 1024
    out_specs = [pl.BlockSpec((tm, tn), lambda i: (i, 0))]
    out_shape = [jax.ShapeDtypeStruct((rows, tn), F32)]
    if epilogue == "heads":
        out_specs.append(pl.BlockSpec((tm * B_HEADS, B_HEAD_DIM), lambda i: (i, 0)))
        out_shape.append(jax.ShapeDtypeStruct((rows * B_HEADS, B_HEAD_DIM), F32))
    out = pl.pallas_call(
        functools.partial(_proj_seg_kernel, epilogue=epilogue),
        grid=(rows // tm,),
        in_specs=[
            pl.BlockSpec((tm, D_MODEL), lambda i: (i, 0)),
            pl.BlockSpec((D_MODEL, tn), lambda i: (0, seg), pipeline_mode=pl.Buffered(1)),
            pl.BlockSpec((1, tn), lambda i: (0, 0)),
            pl.BlockSpec((1, tn), lambda i: (0, 0)),
        ],
        out_specs=out_specs,
        out_shape=out_shape,
        scratch_shapes=[pltpu.VMEM((D_MODEL, tn), BF16)],
        compiler_params=_params("arbitrary"),
        name="proj_" + epilogue,
    )(h, w, ln_g.reshape(1, tn), ln_b.reshape(1, tn))
    return out if epilogue == "heads" else out[0]


def _a_mix_kernel(u_ref, v_ref, z_ref, w_ref, bst_ref, o_ref, *, seq_len, tm):
    row = lax.broadcasted_iota(jnp.int32, (A_CHUNK, A_CHUNK), 0)
    col = lax.broadcasted_iota(jnp.int32, (A_CHUNK, A_CHUNK), 1)
    keep = jnp.where(col <= row, 1.0, 0.0)
    if seq_len < A_CHUNK:
        keep = keep * jnp.where(_div_pow2(row, seq_len) == _div_pow2(col, seq_len), 1.0, 0.0)
    gw = D_A // A_GROUPS
    for g in range(A_GROUPS):
        wg = (w_ref[g] * keep).astype(BF16)
        bias = bst_ref[:, g:g + 1]
        for c in range(tm // A_CHUNK):
            rs = slice(c * A_CHUNK, (c + 1) * A_CHUNK)
            cs = slice(g * gw, (g + 1) * gw)
            mix = jnp.dot(wg, v_ref[rs, cs].astype(BF16), preferred_element_type=F32) + bias
            o_ref[rs, cs] = ((u_ref[rs, cs] * mix) * z_ref[rs, cs]).astype(o_ref.dtype)


def _a_mix(u, vn, z, ws, bs, seq_len, tm):
    rows = u.shape[0]
    n = min(seq_len, A_CHUNK)
    reps = A_CHUNK // n
    wmat = jnp.tile(ws[:, :n, :n], (1, reps, reps))
    bst = jnp.tile(bs[:, :n], (1, reps)).T
    spec = pl.BlockSpec((tm, D_A), lambda i: (i, 0))
    return pl.pallas_call(
        functools.partial(_a_mix_kernel, seq_len=n, tm=tm),
        grid=(rows // tm,),
        in_specs=[spec, spec, spec,
                  pl.BlockSpec((A_GROUPS, A_CHUNK, A_CHUNK), lambda i: (0, 0, 0)),
                  pl.BlockSpec((A_CHUNK, A_GROUPS), lambda i: (0, 0))],
        out_specs=spec,
        out_shape=jax.ShapeDtypeStruct((rows, D_A), BF16),
        compiler_params=_params("arbitrary"),
        name="a_mix",
    )(u, vn, z, wmat, bst)


def _rank_select(g, valid, index, axis, count):
    gm = jnp.where(valid, g, NEG)
    n = g.shape[axis]
    rank = jnp.zeros(g.shape, F32)
    for m in range(n):
        ref = gm[m:m + 1, :] if axis == 0 else gm[:, m:m + 1]
        beats = jnp.where(ref > gm, 1.0, jnp.where(ref == gm, jnp.where(m < index, 1.0, 0.0), 0.0))
        rank = rank + beats
    return jnp.where(valid, jnp.where(rank < float(count), 1.0, 0.0), 0.0), rank


POS_SPLIT = 16


def _moba_prompt_kernel(slope_ref, q_ref, k_ref, v_ref, z_ref, o_ref, qa_scr, ka_scr, va_scr, s_scr, p_scr,
                        *, seq):
    nb = seq // MOBA_BLOCK
    assert nb + 2 <= LANES and seq // POS_SPLIT <= 256
    slope = slope_ref[pl.program_id(1)]

    @pl.when((pl.program_id(0) == 0) & (pl.program_id(1) == 0))
    def _():
        pos = lax.broadcasted_iota(jnp.int32, (seq, LANES), 0)
        lane = lax.broadcasted_iota(jnp.int32, (seq, LANES), 1)
        onehot = jnp.where(lane == _div_pow2(pos, MOBA_BLOCK), 1.0, 0.0)
        hi = _div_pow2(pos, POS_SPLIT).astype(F32)
        lo = (pos - _div_pow2(pos, POS_SPLIT) * POS_SPLIT).astype(F32)
        aug = jnp.where(lane < nb, onehot, jnp.where(lane == nb, hi, jnp.where(lane == nb + 1, lo, 0.0)))
        ka_scr[:, B_HEAD_DIM:] = aug.astype(BF16)
        va_scr[:, B_HEAD_DIM:] = jnp.ones((seq, LANES), BF16)

    k = k_ref[0]
    ka_scr[:, 0:B_HEAD_DIM] = k.astype(BF16)
    va_scr[:, 0:B_HEAD_DIM] = v_ref[0].astype(BF16)

    kbar = jnp.mean(k.reshape(nb, MOBA_BLOCK, B_HEAD_DIM), axis=1)
    qf = q_ref[0] * QK_SCALE
    gt = lax.dot_general(kbar, qf, NT_DIMS, precision=HIGHEST, preferred_element_type=F32)
    blk = lax.broadcasted_iota(jnp.int32, (nb, seq), 0)
    own = _div_pow2(lax.broadcasted_iota(jnp.int32, (nb, seq), 1), MOBA_BLOCK)
    sel, _ = _rank_select(gt, blk < own, blk, 0, MOBA_TOPK)
    pen = jnp.where(sel > 0.5, 0.0, jnp.where(blk == own, 0.0, NEG))
    pen = jnp.concatenate([pen, jnp.zeros((LANES - nb, seq), F32)], axis=0)
    row = lax.broadcasted_iota(jnp.int32, (LANES, seq), 0)
    aug_t = jnp.where(row == nb, POS_SPLIT * slope, jnp.where(row == nb + 1, slope, pen))
    qa_scr[:, B_HEAD_DIM:] = aug_t.T.astype(BF16)
    qa_scr[:, 0:B_HEAD_DIM] = qf.astype(BF16)

    causal = (lax.broadcasted_iota(jnp.int32, (MOBA_BLOCK, MOBA_BLOCK), 0)
              >= lax.broadcasted_iota(jnp.int32, (MOBA_BLOCK, MOBA_BLOCK), 1))
    offs = [MOBA_BLOCK * (jb * (jb + 1) // 2) for jb in range(nb + 1)]
    for jb in range(nb):
        rows = slice(jb * MOBA_BLOCK, (jb + 1) * MOBA_BLOCK)
        nk = (jb + 1) * MOBA_BLOCK
        s_scr[:, offs[jb]:offs[jb + 1]] = lax.dot_general(
            qa_scr[rows, :], ka_scr[0:nk, :], NT_DIMS, preferred_element_type=F32)
    for jb in range(nb):
        n_past = jb * MOBA_BLOCK
        own_cols = slice(offs[jb] + n_past, offs[jb + 1])
        own_s = jnp.where(causal, s_scr[:, own_cols], NEG)
        m = jnp.max(own_s, axis=-1, keepdims=True)
        if jb > 0:
            past_cols = slice(offs[jb], offs[jb] + n_past)
            m = jnp.maximum(m, jnp.max(s_scr[:, past_cols], axis=-1, keepdims=True))
            p_scr[:, past_cols] = jnp.exp(s_scr[:, past_cols] - m).astype(BF16)
        p_scr[:, own_cols] = jnp.exp(own_s - m).astype(BF16)
    for jb in range(nb):
        rows = slice(jb * MOBA_BLOCK, (jb + 1) * MOBA_BLOCK)
        nk = (jb + 1) * MOBA_BLOCK
        o = jnp.dot(p_scr[:, offs[jb]:offs[jb + 1]], va_scr[0:nk, :], preferred_element_type=F32)
        y = o[:, 0:B_HEAD_DIM] / o[:, B_HEAD_DIM:B_HEAD_DIM + 1]
        o_ref[0, rows, :] = (y * z_ref[0, rows, :]).astype(o_ref.dtype)


def _alibi_slopes():
    return jnp.asarray(np.array([2.0 ** (-8.0 * (h + 1) / B_HEADS) for h in range(B_HEADS)], np.float32))


def _moba_prompt(q, k, v, zs, batch, seq):
    shp = (batch, seq, D_B)
    spec = pl.BlockSpec((1, seq, B_HEAD_DIM), lambda b, h: (b, 0, h))
    nb = seq // MOBA_BLOCK
    n_score_cols = MOBA_BLOCK * (nb * (nb + 1) // 2)
    out = pl.pallas_call(
        functools.partial(_moba_prompt_kernel, seq=seq),
        grid=(batch, B_HEADS),
        in_specs=[pl.BlockSpec(memory_space=pltpu.SMEM), spec, spec, spec, spec],
        out_specs=spec,
        out_shape=jax.ShapeDtypeStruct(shp, BF16),
        scratch_shapes=[pltpu.VMEM((seq, B_HEAD_DIM + LANES), BF16)] * 3
                       + [pltpu.VMEM((MOBA_BLOCK, n_score_cols), F32),
                          pltpu.VMEM((MOBA_BLOCK, n_score_cols), BF16)],
        compiler_params=_params("arbitrary", "arbitrary"),
        name="moba_prompt",
    )(_alibi_slopes(), q.reshape(shp), k.reshape(shp), v.reshape(shp), zs.reshape(shp))
    return out.reshape(batch * seq, D_B)


CHUNK_PAGES = 8
K_RING = 4


def _moba_sample_kernel(pt_ref, slope_ref, trow_ref, q_ref, kn_ref, vn_ref, z_ref, ck_ref, cv_ref, o_ref,
                        kbuf, vbuf, ksem, vsem, isem, idx_v, idx_s,
                        qbd_scr, qbf_scr, cat_scr, s_scr, p_scr, ksum_scr, onew_scr, l_scr,
                        *, n_seq, n_new, n_pages):
    pg = CHUNK_PAGES
    nch = n_pages // pg
    total = n_seq * nch
    n_blocks = n_pages * PAGE_SIZE // MOBA_BLOCK
    blocks_per_chunk = pg * PAGE_SIZE // MOBA_BLOCK
    pages_per_block = MOBA_BLOCK // PAGE_SIZE
    rows = n_new * B_HEADS
    t = pl.program_id(0)
    head_of_lane = _div_pow2(lax.broadcasted_iota(jnp.int32, (B_HEADS, D_B), 1), B_HEAD_DIM)
    diag = head_of_lane == lax.broadcasted_iota(jnp.int32, (B_HEADS, D_B), 0)
    slope = slope_ref[:, 0:1]
    trow = trow_ref[:, 0:1]

    def k_copy(page, slot, r):
        return pltpu.make_async_copy(ck_ref.at[page], kbuf.at[slot, r], ksem.at[slot])

    def start_chunk(seq, ch, slot):
        for r in range(pg):
            k_copy(pt_ref[seq, ch * pg + r], slot, r).start()

    def v_copy(page, r, j, half):
        dst = vbuf.at[r, pl.ds((j * pages_per_block + half) * PAGE_SIZE, PAGE_SIZE), :]
        return pltpu.make_async_copy(cv_ref.at[page, :, r % B_HEADS, :], dst, vsem.at[0])

    @pl.when(t == 0)
    def _():
        for f0 in range(K_RING - 1):
            start_chunk(0, f0, f0)

    @pl.when(t < n_seq)
    def _():
        tiles = []
        for qi in range(n_new):
            qrow = q_ref[0, qi:qi + 1, :] * QK_SCALE
            tiles.append(jnp.where(diag, jnp.broadcast_to(qrow, (B_HEADS, D_B)), 0.0))
        qbd = jnp.concatenate(tiles, axis=0)
        qbf_scr[...] = qbd
        qbd_scr[...] = qbd.astype(BF16)

        def chunk_body(c, carry):
            f = t * nch + c
            ahead = f + (K_RING - 1)

            @pl.when(ahead < total)
            def _():
                start_chunk(lax.div(ahead, jnp.int32(nch)), lax.rem(ahead, jnp.int32(nch)),
                            lax.rem(ahead, jnp.int32(K_RING)))

            slot = lax.rem(f, jnp.int32(K_RING))
            for r in range(pg):
                k_copy(0, slot, r).wait()
            pair = None
            for r in range(pg):
                for hd in range(B_HEADS):
                    part = kbuf[slot, r, pl.ds(hd, PAGE_SIZE, stride=B_HEADS), :]
                    cat_scr[r * PAGE_SIZE:(r + 1) * PAGE_SIZE,
                            hd * B_HEAD_DIM:(hd + 1) * B_HEAD_DIM] = part.astype(BF16)
                ps = jnp.sum(kbuf[slot, r].reshape(PAGE_SIZE, B_HEADS, B_HEAD_DIM), axis=0)
                if r % pages_per_block == 0:
                    pair = ps
                else:
                    pair = pair + ps
                if r % pages_per_block == pages_per_block - 1:
                    ksum_scr[c * blocks_per_chunk + r // pages_per_block] = pair
            s_scr[c] = lax.dot_general(qbd_scr[...], cat_scr[...], NT_DIMS, preferred_element_type=F32)
            return carry

        lax.fori_loop(0, nch, chunk_body, 0)

    @pl.when(t >= 1)
    def _():
        for r in range(rows):
            for j in range(MOBA_TOPK):
                for half in range(pages_per_block):
                    v_copy(0, r, j, half).wait()
        head_rows = []
        for r in range(rows):
            parts = [p_scr[idx_s[r, j], r:r + 1, :] for j in range(MOBA_TOPK)]
            p_r = jnp.broadcast_to(jnp.concatenate(parts, axis=1), (SUBLANES, MOBA_TOPK * MOBA_BLOCK))
            o = jnp.dot(p_r.astype(BF16), vbuf[r].astype(BF16), preferred_element_type=F32)
            head_rows.append(o[0:1, :])
        out_rows = []
        for qi in range(n_new):
            sl = slice(qi * B_HEADS, (qi + 1) * B_HEADS)
            o_sel = jnp.concatenate(head_rows[sl], axis=1)
            o_new = jnp.sum(jnp.where(diag, onew_scr[sl, :], 0.0), axis=0, keepdims=True)
            l_row = jnp.concatenate([l_scr[r:r + 1, :] for r in range(sl.start, sl.stop)], axis=1)
            out_rows.append((o_sel + o_new) / l_row)
        o_ref[0] = jnp.concatenate(out_rows, axis=0) * z_ref[0]

    @pl.when(t < n_seq)
    def _():
        kbar = jnp.concatenate([ksum_scr[:, hd, :] for hd in range(B_HEADS)], axis=1)
        kbar = kbar * (1.0 / MOBA_BLOCK)
        g = lax.dot_general(qbf_scr[...], kbar, NT_DIMS, precision=HIGHEST,
                            preferred_element_type=F32)
        blk = lax.broadcasted_iota(jnp.int32, (rows, n_blocks), 1)
        sel, rank = _rank_select(g, blk >= 0, blk, 1, MOBA_TOPK)

        lane = lax.broadcasted_iota(jnp.int32, (rows, LANES), 1)
        ids = jnp.zeros((rows, LANES), F32)
        for j in range(MOBA_TOPK):
            col = jnp.sum(jnp.where(rank == float(j), blk.astype(F32), 0.0), axis=-1, keepdims=True)
            ids = jnp.where(lane == j, col, ids)
        idx_v[...] = ids.astype(jnp.int32)
        to_smem = pltpu.make_async_copy(idx_v, idx_s, isem.at[0])
        to_smem.start()
        to_smem.wait()
        for r in range(rows):
            for j in range(MOBA_TOPK):
                for half in range(pages_per_block):
                    v_copy(pt_ref[t, idx_s[r, j] * pages_per_block + half], r, j, half).start()

        key = lax.broadcasted_iota(jnp.int32, (rows, MOBA_BLOCK), 1).astype(F32)
        top = None
        for n in range(n_blocks):
            gi, a = divmod(n, blocks_per_chunk)
            dist = trow - (float(n * MOBA_BLOCK) + key)
            slab = s_scr[gi, :, a * MOBA_BLOCK:(a + 1) * MOBA_BLOCK] - slope * dist
            slab = jnp.where(sel[:, n:n + 1] > 0.5, slab, NEG)
            p_scr[n] = slab
            top = slab if top is None else jnp.maximum(top, slab)
        m = jnp.max(top, axis=-1, keepdims=True)
        knew = jnp.concatenate([kn_ref[0], jnp.zeros((LANES - n_new, D_B), F32)], axis=0).astype(BF16)
        s_new = lax.dot_general(qbd_scr[...], knew, NT_DIMS, preferred_element_type=F32)
        dist_new = trow - (float(n_pages * PAGE_SIZE) + lane.astype(F32))
        s_new = jnp.where(dist_new >= 0.0, s_new - slope * dist_new, NEG)
        m = jnp.maximum(m, jnp.max(s_new, axis=-1, keepdims=True))
        tot = jnp.zeros((rows, MOBA_BLOCK), F32)
        for n in range(n_blocks):
            p = jnp.exp(p_scr[n] - m)
            tot = tot + p
            p_scr[n] = p
        pn = jnp.exp(s_new - m)
        l = jnp.sum(tot, axis=-1, keepdims=True) + jnp.sum(pn, axis=-1, keepdims=True)
        l_scr[...] = jnp.broadcast_to(l, (rows, LANES))
        vnew = jnp.concatenate([vn_ref[0], jnp.zeros((LANES - n_new, D_B), F32)], axis=0).astype(BF16)
        onew_scr[...] = jnp.dot(pn.astype(BF16), vnew, preferred_element_type=F32)


def _moba_sample(q, kn, vn, zs, cache_k, cache_v, page_table, n_seq, n_new):
    n_pages = page_table.shape[1]
    n_phys = cache_k.shape[0]
    pg = CHUNK_PAGES
    nch = n_pages // pg
    assert n_pages % pg == 0 and K_RING - 1 <= nch
    rows = n_new * B_HEADS
    past = n_pages * PAGE_SIZE
    shp = (n_seq, n_new, D_B)
    slope_rows = np.tile(np.array([2.0 ** (-8.0 * (h + 1) / B_HEADS) for h in range(B_HEADS)], np.float32),
                         n_new)
    t_rows = np.repeat(past + np.arange(n_new, dtype=np.float32), B_HEADS)
    slope_rows = jnp.asarray(np.tile(slope_rows[:, None], (1, LANES)))
    t_rows = jnp.asarray(np.tile(t_rows[:, None], (1, LANES)))

    const = pl.BlockSpec((rows, LANES), lambda t, pt: (0, 0))
    cur = pl.BlockSpec((1, n_new, D_B), lambda t, pt: (jnp.minimum(t, n_seq - 1), 0, 0))
    prev = pl.BlockSpec((1, n_new, D_B), lambda t, pt: (jnp.maximum(t - 1, 0), 0, 0))
    hbm = pl.BlockSpec(memory_space=pl.ANY)

    page_rows = PAGE_SIZE * B_HEADS
    n_blocks = past // MOBA_BLOCK
    out = pl.pallas_call(
        functools.partial(_moba_sample_kernel, n_seq=n_seq, n_new=n_new, n_pages=n_pages),
        grid_spec=pltpu.PrefetchScalarGridSpec(
            num_scalar_prefetch=1,
            grid=(n_seq + 1,),
            in_specs=[const, const, cur, cur, cur, prev, hbm, hbm],
            out_specs=prev,
            scratch_shapes=[
                pltpu.VMEM((K_RING, pg, page_rows, B_HEAD_DIM), F32),
                pltpu.VMEM((rows, MOBA_TOPK * MOBA_BLOCK, B_HEAD_DIM), F32),
                pltpu.SemaphoreType.DMA((K_RING,)),
                pltpu.SemaphoreType.DMA((1,)),
                pltpu.SemaphoreType.DMA((1,)),
                pltpu.VMEM((rows, LANES), jnp.int32),
                pltpu.SMEM((rows, LANES), jnp.int32),
                pltpu.VMEM((rows, D_B), BF16),
                pltpu.VMEM((rows, D_B), F32),
                pltpu.VMEM((pg * PAGE_SIZE, D_B), BF16),
                pltpu.VMEM((nch, rows, pg * PAGE_SIZE), F32),
                pltpu.VMEM((n_blocks, rows, MOBA_BLOCK), F32),
                pltpu.VMEM((n_blocks, B_HEADS, B_HEAD_DIM), F32),
                pltpu.VMEM((rows, D_B), F32),
                pltpu.VMEM((rows, LANES), F32),
            ]),
        out_shape=jax.ShapeDtypeStruct(shp, F32),
        compiler_params=_params("arbitrary"),
        name="moba_sample",
    )(page_table, slope_rows, t_rows, q.reshape(shp), kn.reshape(shp), vn.reshape(shp), zs.reshape(shp),
      cache_k.reshape(n_phys, page_rows, B_HEAD_DIM), cache_v)
    return out.reshape(n_seq * n_new, D_B)


def _out_proj_kernel(ya_ref, yb_ref, w_ref, x_ref, gt_ref, g_ref, *rest, final):
    wbf_ref = rest[-1]

    @pl.when(pl.program_id(0) == 0)
    def _():
        wbf_ref[...] = w_ref[...].astype(BF16)

    half = ya_ref.shape[1]
    acc = jnp.dot(ya_ref[...].astype(BF16), wbf_ref[0:half, :], preferred_element_type=F32)
    acc = acc + jnp.dot(yb_ref[...].astype(BF16), wbf_ref[half:, :], preferred_element_type=F32)
    x = x_ref[...] + gt_ref[0] * acc
    r = lax.rsqrt(jnp.mean(x * x, axis=-1, keepdims=True) + EPS)
    if final:
        o_ref = rest[0]
        o_ref[...] = (x * r) * g_ref[...]
    else:
        sc_ref, sh_ref, x_out_ref, h_ref = rest[:4]
        x_out_ref[...] = x
        h_ref[...] = ((x * r) * g_ref[...] * (1.0 + sc_ref[0]) + sh_ref[0]).astype(h_ref.dtype)


def _out_proj(ya, yb, ya_col, yb_col, w, x, mod, g, next_mod, tm):
    rows = x.shape[0]
    half = D_MODEL // 2
    final = next_mod is None
    row_f32 = pl.BlockSpec((tm, D_MODEL), lambda i: (i, 0))
    in_specs = [
        pl.BlockSpec((tm, half), lambda i: (i, ya_col)),
        pl.BlockSpec((tm, half), lambda i: (i, yb_col)),
        pl.BlockSpec((D_MODEL, D_MODEL), lambda i: (0, 0), pipeline_mode=pl.Buffered(1)),
        row_f32,
        mod.spec(2, tm),
        pl.BlockSpec((1, D_MODEL), lambda i: (0, 0)),
    ]
    args = [ya, yb, w, x, mod.arr, g.reshape(1, D_MODEL)]
    if final:
        out_specs, out_shape = row_f32, jax.ShapeDtypeStruct((rows, D_MODEL), F32)
    else:
        in_specs += [next_mod.spec(1, tm), next_mod.spec(0, tm)]
        args += [next_mod.arr, next_mod.arr]
        out_specs = [row_f32, row_f32]
        out_shape = [jax.ShapeDtypeStruct((rows, D_MODEL), F32), jax.ShapeDtypeStruct((rows, D_MODEL), BF16)]
    return pl.pallas_call(
        functools.partial(_out_proj_kernel, final=final),
        grid=(rows // tm,),
        in_specs=in_specs,
        out_specs=out_specs,
        out_shape=out_shape,
        scratch_shapes=[pltpu.VMEM((D_MODEL, D_MODEL), BF16)],
        compiler_params=_params("arbitrary"),
        name="out_proj_final" if final else "out_proj",
    )(*args)


def _conv_proj_kernel(h_ref, wa_ref, wb_ref, wz_ref, glu_ref, sz_ref, wbf_ref):
    @pl.when(pl.program_id(1) == 0)
    def _():
        wbf_ref[0] = wa_ref[...].astype(BF16)
        wbf_ref[1] = wb_ref[...].astype(BF16)
        wbf_ref[2] = wz_ref[...].astype(BF16)

    h = h_ref[...]
    a = jnp.dot(h, wbf_ref[0], preferred_element_type=F32)
    b = jnp.dot(h, wbf_ref[1], preferred_element_type=F32)
    glu_ref[...] = a * _sigmoid(b)
    sz_ref[...] = _silu(jnp.dot(h, wbf_ref[2], preferred_element_type=F32))


def _conv_proj(h, w, tm):
    rows = h.shape[0]
    tn = 512
    nj = D_C // tn
    out = pl.BlockSpec((tm, tn), lambda j, i: (i, j))
    return pl.pallas_call(
        _conv_proj_kernel,
        grid=(nj, rows // tm),
        in_specs=[
            pl.BlockSpec((tm, D_MODEL), lambda j, i: (i, 0)),
            pl.BlockSpec((D_MODEL, tn), lambda j, i: (0, j)),
            pl.BlockSpec((D_MODEL, tn), lambda j, i: (0, nj + j)),
            pl.BlockSpec((D_MODEL, tn), lambda j, i: (0, 2 * nj + j)),
        ],
        out_specs=[out, out],
        out_shape=[jax.ShapeDtypeStruct((rows, D_C), F32)] * 2,
        scratch_shapes=[pltpu.VMEM((3, D_MODEL, tn), BF16)],
        compiler_params=_params("arbitrary", "arbitrary"),
        name="conv_proj",
    )(h, w, w, w)


HALO = 32
CONV_LANES = 128


def _conv_ln_kernel(glu_ref, halo_ref, sz_ref, w_ref, cb_ref, g_ref, b_ref, o_ref, buf_scr, y_scr, wb_scr,
                    *, tiles_per_seq, tm):
    first = (pl.program_id(0) % tiles_per_seq) == 0

    @pl.when(pl.program_id(0) == 0)
    def _():
        for k in range(CONV_W):
            wb_scr[k] = jnp.broadcast_to(w_ref[k:k + 1, :], (SUBLANES, D_C))

    @pl.when(first)
    def _():
        buf_scr[0:HALO, :] = jnp.zeros((HALO, D_C), F32)

    @pl.when(jnp.logical_not(first))
    def _():
        buf_scr[0:HALO, :] = halo_ref[...]

    buf_scr[HALO:HALO + tm, :] = glu_ref[...]

    off = HALO - (CONV_W - 1)
    taps = [[(a, SUBLANES * a + r - off) for a in range((CONV_W + off) // SUBLANES + 1)
             if 0 <= SUBLANES * a + r - off < CONV_W] for r in range(SUBLANES)]
    rowid = lax.broadcasted_iota(jnp.int32, (SUBLANES, CONV_LANES), 0)

    for lc in range(D_C // CONV_LANES):
        lanes = slice(lc * CONV_LANES, (lc + 1) * CONV_LANES)

        def z_block(u0, r):
            acc = None
            for a, k in taps[r]:
                term = wb_scr[k, :, lanes] * buf_scr[pl.ds(u0 + SUBLANES * a, SUBLANES), lanes]
                acc = term if acc is None else acc + term
            return acc

        def body(blk, carry):
            t0 = pl.multiple_of(blk * SUBLANES, SUBLANES)
            terms = [z_block(t0, 0)]
            nxt = []
            for r in range(1, SUBLANES):
                zn = z_block(t0 + SUBLANES, r)
                mixed = jnp.where(rowid >= r, carry[r - 1], zn)
                terms.append(pltpu.roll(mixed, SUBLANES - r, 0))
                nxt.append(zn)
            while len(terms) > 1:
                terms = [terms[i] + terms[i + 1] for i in range(0, len(terms), 2)]
            y_scr[pl.ds(t0, SUBLANES), lanes] = terms[0]
            return tuple(nxt)

        init = tuple(z_block(0, r) for r in range(1, SUBLANES))
        lax.fori_loop(0, tm // SUBLANES, body, init, unroll=4)

    y = _layernorm_rows(y_scr[...] + cb_ref[...], g_ref[...], b_ref[...])
    o_ref[...] = (_silu(y) * sz_ref[...]).astype(o_ref.dtype)


def _conv_ln(glu, sz, conv_w, conv_b, ln_g, ln_b, seq, tm):
    rows = glu.shape[0]
    vec = pl.BlockSpec((1, D_C), lambda i: (0, 0))
    tile = pl.BlockSpec((tm, D_C), lambda i: (i, 0))
    return pl.pallas_call(
        functools.partial(_conv_ln_kernel, tiles_per_seq=seq // tm, tm=tm),
        grid=(rows // tm,),
        in_specs=[tile,
                  pl.BlockSpec((HALO, D_C), lambda i: (jnp.maximum(i * (tm // HALO) - 1, 0), 0)),
                  tile,
                  pl.BlockSpec((CONV_W, D_C), lambda i: (0, 0)),
                  vec, vec, vec],
        out_specs=tile,
        out_shape=jax.ShapeDtypeStruct((rows, D_C), BF16),
        scratch_shapes=[pltpu.VMEM((HALO + tm, D_C), F32), pltpu.VMEM((tm, D_C), F32),
                        pltpu.VMEM((CONV_W, SUBLANES, D_C), F32)],
        compiler_params=_params("arbitrary"),
        name="conv_ln",
    )(glu, glu, sz, conv_w, conv_b.reshape(1, D_C), ln_g.reshape(1, D_C), ln_b.reshape(1, D_C))


def _conv_ln_sample_kernel(st_ref, glu_ref, sz_ref, w_ref, cb_ref, g_ref, b_ref, o_ref, ns_ref, buf_scr,
                           *, n_new):
    hist = CONV_W - 1
    buf_scr[0:hist, :] = st_ref[0]
    buf_scr[hist:hist + n_new, :] = glu_ref[0]
    acc = jnp.zeros((n_new, D_C), F32) + cb_ref[...]
    for k in range(CONV_W):
        acc = acc + w_ref[k:k + 1, :] * buf_scr[k:k + n_new, :]
    y = _layernorm_rows(acc, g_ref[...], b_ref[...])
    o_ref[0] = _silu(y) * sz_ref[0]
    ns_ref[0] = buf_scr[n_new:n_new + hist, :]


def _conv_ln_sample(state, glu, sz, conv_w, conv_b, ln_g, ln_b, n_seq, n_new):
    hist = CONV_W - 1
    vec = pl.BlockSpec((1, D_C), lambda s: (0, 0))
    new = pl.BlockSpec((1, n_new, D_C), lambda s: (s, 0, 0))
    st = pl.BlockSpec((1, hist, D_C), lambda s: (s, 0, 0))
    y, ns = pl.pallas_call(
        functools.partial(_conv_ln_sample_kernel, n_new=n_new),
        grid=(n_seq,),
        in_specs=[st, new, new, pl.BlockSpec((CONV_W, D_C), lambda s: (0, 0)), vec, vec, vec],
        out_specs=[new, st],
        out_shape=[jax.ShapeDtypeStruct((n_seq, n_new, D_C), F32),
                   jax.ShapeDtypeStruct((n_seq, hist, D_C), F32)],
        scratch_shapes=[pltpu.VMEM((hist + n_new, D_C), F32)],
        compiler_params=_params("arbitrary"),
        name="conv_ln_sample",
    )(state, glu.reshape(n_seq, n_new, D_C), sz.reshape(n_seq, n_new, D_C), conv_w,
      conv_b.reshape(1, D_C), ln_g.reshape(1, D_C), ln_b.reshape(1, D_C))
    return y.reshape(n_seq * n_new, D_C), ns


def _trunk(x, mods, tm, seq, weights, attend, conv):
    (norm_g, ab_w_in, a_ln_g, a_ln_b, a_ws, a_bs, ab_w_out, c_w_in, c_w_out, final_norm_g) = weights
    tm_row = min(tm, ROW_TILE)
    h = _norm_mod(x, norm_g[0], mods[0], tm_row)
    seg = lambda s, ep: _proj_seg(h, ab_w_in[0], s, ep, a_ln_g[0], a_ln_b[0], tm)
    u, vn, za = seg(0, "gelu"), seg(1, "gelu_ln"), seg(2, "silu")
    q, zb = seg(3, "none"), seg(6, "silu")
    (k, k_heads), (v, v_heads) = seg(4, "heads"), seg(5, "heads")
    ya = _a_mix(u, vn, za, a_ws[0], a_bs[0], seq, tm_row)
    yb = attend(q, k, v, zb)
    x1, h1 = _out_proj(ya, yb, 0, 0, ab_w_out[0], x, mods[0], norm_g[1], mods[1], tm_row)
    glu, sz = _conv_proj(h1, c_w_in[0], tm)
    y2, conv_state = conv(glu, sz)
    y = _out_proj(y2, y2, 0, 1, c_w_out[0], x1, mods[1], final_norm_g, None, tm_row)
    return y, k_heads, v_heads, vn, conv_state


def kernel(x_prompt, x_sample, cache_b_k, cache_b_v, state_c_conv, page_table, c_prompt, c_sample,
           ada_w, ada_b, norm_g, ab_w_in, a_ln_g, a_ln_b, a_ws, a_bs, ab_w_out,
           c_w_in, c_conv_w, c_conv_b, c_ln_g, c_ln_b, c_w_out, final_norm_g):
    batch, seq, _ = x_prompt.shape
    n_seq, n_new, _ = x_sample.shape
    weights = (norm_g, ab_w_in, a_ln_g, a_ln_b, a_ws, a_bs, ab_w_out, c_w_in, c_w_out, final_norm_g)

    mod = _adaln(jnp.concatenate([c_prompt, c_sample], axis=0), ada_w, ada_b)
    mod = mod.reshape(DEPTH, batch + n_seq, 3, D_MODEL)
    tm_p = PROJ_TILE
    mods_p, mods_s = [], []
    for l in range(DEPTH):
        mods_p.append(_Mod(mod[l, :batch].reshape(batch * 3, 1, D_MODEL), seq))
        per_row = jnp.repeat(mod[l, batch:], n_new, axis=0).transpose(1, 0, 2)
        mods_s.append(_Mod(per_row))

    yp, kp, vp, _, glu_p = _trunk(
        x_prompt.reshape(batch * seq, D_MODEL), mods_p, tm_p, seq, weights,
        attend=lambda q, k, v, zb: _moba_prompt(q, k, v, zb, batch, seq),
        conv=lambda glu, sz: (_conv_ln(glu, sz, c_conv_w[0], c_conv_b[0], c_ln_g[0], c_ln_b[0], seq, ROW_TILE), glu))
    ys, ks, vs, vas, conv_s = _trunk(
        x_sample.reshape(n_seq * n_new, D_MODEL), mods_s, n_seq * n_new, n_new, weights,
        attend=lambda q, k, v, zb: _moba_sample(q, k, v, zb, cache_b_k[0], cache_b_v[0], page_table,
                                               n_seq, n_new),
        conv=lambda glu, sz: _conv_ln_sample(state_c_conv[0], glu, sz, c_conv_w[0], c_conv_b[0],
                                             c_ln_g[0], c_ln_b[0], n_seq, n_new))

    hd = (B_HEADS, B_HEAD_DIM)
    conv_p = glu_p.reshape(batch, seq, D_C)[:, seq - (CONV_W - 1):, :]
    return (yp.reshape(batch, seq, D_MODEL), ys.reshape(n_seq, n_new, D_MODEL),
            kp.reshape(1, batch, seq, *hd), vp.reshape(1, batch, seq, *hd),
            ks.reshape(1, n_seq, n_new, *hd), vs.reshape(1, n_seq, n_new, *hd),
            vas.reshape(1, n_seq, n_new, D_A), conv_p[None], conv_s[None])
```

```python
import functools

import numpy as np
import jax
import jax.numpy as jnp
from jax import lax
from jax.experimental import pallas as pl
from jax.experimental.pallas import tpu as pltpu

F32 = jnp.float32
BF16 = jnp.bfloat16
HIGHEST = lax.Precision.HIGHEST

D_MODEL = 2048
DEPTH = 2
PAST_LEN = 8192
PAGE_SIZE = 128
D_A = 1024
A_GROUPS = 8
A_CHUNK = 128
B_HEADS = 8
B_HEAD_DIM = 128
D_B = B_HEADS * B_HEAD_DIM
MOBA_BLOCK = 256
MOBA_TOPK = 3
D_C = D_MODEL
CONV_W = 31
EPS = 1e-6
NEG = -1e30
QK_SCALE = B_HEAD_DIM ** -0.5

VMEM_LIMIT_BYTES = 56 * 1024 * 1024
SUBLANES = 8
LANES = 128
PROJ_TILE = 1024
ROW_TILE = 512
OUT_SUB_ROWS = 256

NT_DIMS = (((1,), (1,)), ((), ()))


def _params(*sem):
    return pltpu.CompilerParams(dimension_semantics=sem, vmem_limit_bytes=VMEM_LIMIT_BYTES)


def _div_pow2(x, d):
    shift = int(d).bit_length() - 1
    assert 1 << shift == d
    return lax.shift_right_logical(x, jnp.int32(shift))


def _sigmoid(x):
    return jax.nn.sigmoid(x)


def _silu(x):
    return x * _sigmoid(x)


def _gelu_tanh(x):
    c = np.float32(np.sqrt(2.0 / np.pi))
    return x * (0.5 * (1.0 + jnp.tanh(c * (x + 0.044715 * (x * x * x)))))


def _layernorm_rows(x, g, b):
    mu = jnp.mean(x, axis=-1, keepdims=True)
    xc = x - mu
    var = jnp.mean(xc * xc, axis=-1, keepdims=True)
    return (xc * lax.rsqrt(var + EPS)) * g + b


def _adaln_kernel(c_ref, w_ref, b_ref, o_ref):
    s = _silu(c_ref[...]).astype(BF16)
    o_ref[0] = jnp.dot(s, w_ref[0].astype(BF16), preferred_element_type=F32) + b_ref[0]


def _adaln(c_all, ada_w, ada_b):
    n = c_all.shape[0]
    tn = 1024
    return pl.pallas_call(
        _adaln_kernel,
        grid=(DEPTH, 3 * D_MODEL // tn),
        in_specs=[
            pl.BlockSpec((n, D_MODEL), lambda l, j: (0, 0)),
            pl.BlockSpec((1, D_MODEL, tn), lambda l, j: (l, 0, j)),
            pl.BlockSpec((1, 1, tn), lambda l, j: (l, 0, j)),
        ],
        out_specs=pl.BlockSpec((1, n, tn), lambda l, j: (l, 0, j)),
        out_shape=jax.ShapeDtypeStruct((DEPTH, n, 3 * D_MODEL), F32),
        compiler_params=_params("arbitrary", "arbitrary"),
        name="adaln",
    )(c_all, ada_w, ada_b.reshape(DEPTH, 1, 3 * D_MODEL))


class _Mod:
    def __init__(self, arr, seq_len=None):
        self.arr = arr
        self.rows = arr.shape[1]
        self.seq_len = seq_len

    def spec(self, which, tm):
        if self.seq_len is None:
            return pl.BlockSpec((1, self.rows, D_MODEL), lambda i: (which, 0, 0))
        assert self.seq_len % tm == 0
        tiles = self.seq_len // tm
        return pl.BlockSpec((1, 1, D_MODEL), lambda i: ((i // tiles) * 3 + which, 0, 0))


def _mod_rows(ref, rs):
    return ref[0] if ref.shape[1] == 1 else ref[0, rs, :]


def _proj_norm_kernel(x_ref, ng_ref, sc_ref, sh_ref, w_ref, u_ref, h_ref, wbf_ref):
    @pl.when(pl.program_id(0) == 0)
    def _():
        wbf_ref[...] = w_ref[...].astype(BF16)

    tm = x_ref.shape[0]
    sub = min(tm, OUT_SUB_ROWS)
    for c in range(tm // sub):
        rs = slice(c * sub, (c + 1) * sub)
        x = x_ref[rs, :]
        r = lax.rsqrt(jnp.mean(x * x, axis=-1, keepdims=True) + EPS)
        h = ((x * r) * ng_ref[...] * (1.0 + _mod_rows(sc_ref, rs)) + _mod_rows(sh_ref, rs)).astype(BF16)
        h_ref[rs, :] = h
        u_ref[rs, :] = _gelu_tanh(jnp.dot(h, wbf_ref[...], preferred_element_type=F32))


def _proj_norm(x, g, mod, w, tm):
    rows = x.shape[0]
    tn = 1024
    return pl.pallas_call(
        _proj_norm_kernel,
        grid=(rows // tm,),
        in_specs=[
            pl.BlockSpec((tm, D_MODEL), lambda i: (i, 0)),
            pl.BlockSpec((1, D_MODEL), lambda i: (0, 0)),
            mod.spec(1, tm),
            mod.spec(0, tm),
            pl.BlockSpec((D_MODEL, tn), lambda i: (0, 0), pipeline_mode=pl.Buffered(1)),
        ],
        out_specs=[pl.BlockSpec((tm, tn), lambda i: (i, 0)), pl.BlockSpec((tm, D_MODEL), lambda i: (i, 0))],
        out_shape=[jax.ShapeDtypeStruct((rows, tn), F32), jax.ShapeDtypeStruct((rows, D_MODEL), BF16)],
        scratch_shapes=[pltpu.VMEM((D_MODEL, tn), BF16)],
        compiler_params=_params("arbitrary"),
        name="proj_norm_gelu",
    )(x, g.reshape(1, D_MODEL), mod.arr, mod.arr, w)


def _proj_seg_kernel(h_ref, w_ref, g_ref, b_ref, *rest, epilogue):
    o_ref, wbf_ref = rest[0], rest[-1]

    @pl.when(pl.program_id(0) == 0)
    def _():
        wbf_ref[...] = w_ref[...].astype(BF16)

    acc = jnp.dot(h_ref[...], wbf_ref[...], preferred_element_type=F32)
    if epilogue == "gelu_ln":
        acc = _layernorm_rows(_gelu_tanh(acc), g_ref[...], b_ref[...])
    elif epilogue == "silu":
        acc = _silu(acc)
    o_ref[...] = acc
    if epilogue == "heads":
        oh_ref = rest[1]
        tm = acc.shape[0]
        for hd in range(B_HEADS):
            oh_ref[pl.ds(hd, tm, stride=B_HEADS), :] = acc[:, hd * B_HEAD_DIM:(hd + 1) * B_HEAD_DIM]


def _proj_seg(h, w, seg, epilogue, ln_g, ln_b, tm):
    rows = h.shape[0]
    tn = 1024
    out_specs = [pl.BlockSpec((tm, tn), lambda i: (i, 0))]
    out_shape = [jax.ShapeDtypeStruct((rows, tn), F32)]
    if epilogue == "heads":
        out_specs.append(pl.BlockSpec((tm * B_HEADS, B_HEAD_DIM), lambda i: (i, 0)))
        out_shape.append(jax.ShapeDtypeStruct((rows * B_HEADS, B_HEAD_DIM), F32))
    out = pl.pallas_call(
        functools.partial(_proj_seg_kernel, epilogue=epilogue),
        grid=(rows // tm,),
        in_specs=[
            pl.BlockSpec((tm, D_MODEL), lambda i: (i, 0)),
            pl.BlockSpec((D_MODEL, tn), lambda i: (0, seg), pipeline_mode=pl.Buffered(1)),
            pl.BlockSpec((1, tn), lambda i: (0, 0)),
            pl.BlockSpec((1, tn), lambda i: (0, 0)),
        ],
        out_specs=out_specs,
        out_shape=out_shape,
        scratch_shapes=[pltpu.VMEM((D_MODEL, tn), BF16)],
        compiler_params=_params("arbitrary"),
        name="proj_" + epilogue,
    )(h, w, ln_g.reshape(1, tn), ln_b.reshape(1, tn))
    return out if epilogue == "heads" else out[0]


def _a_mix_kernel(u_ref, v_ref, z_ref, w_ref, bst_ref, o_ref, *, seq_len, tm):
    row = lax.broadcasted_iota(jnp.int32, (A_CHUNK, A_CHUNK), 0)
    col = lax.broadcasted_iota(jnp.int32, (A_CHUNK, A_CHUNK), 1)
    keep = jnp.where(col <= row, 1.0, 0.0)
    if seq_len < A_CHUNK:
        keep = keep * jnp.where(_div_pow2(row, seq_len) == _div_pow2(col, seq_len), 1.0, 0.0)
    gw = D_A // A_GROUPS
    for g in range(A_GROUPS):
        wg = (w_ref[g] * keep).astype(BF16)
        bias = bst_ref[:, g:g + 1]
        for c in range(tm // A_CHUNK):
            rs = slice(c * A_CHUNK, (c + 1) * A_CHUNK)
            cs = slice(g * gw, (g + 1) * gw)
            mix = jnp.dot(wg, v_ref[rs, cs].astype(BF16), preferred_element_type=F32) + bias
            o_ref[rs, cs] = ((u_ref[rs, cs] * mix) * z_ref[rs, cs]).astype(o_ref.dtype)


def _a_mix(u, vn, z, ws, bs, seq_len, tm):
    rows = u.shape[0]
    n = min(seq_len, A_CHUNK)
    reps = A_CHUNK // n
    wmat = jnp.tile(ws[:, :n, :n], (1, reps, reps))
    bst = jnp.tile(bs[:, :n], (1, reps)).T
    spec = pl.BlockSpec((tm, D_A), lambda i: (i, 0))
    return pl.pallas_call(
        functools.partial(_a_mix_kernel, seq_len=n, tm=tm),
        grid=(rows // tm,),
        in_specs=[spec, spec, spec,
                  pl.BlockSpec((A_GROUPS, A_CHUNK, A_CHUNK), lambda i: (0, 0, 0)),
                  pl.BlockSpec((A_CHUNK, A_GROUPS), lambda i: (0, 0))],
        out_specs=spec,
        out_shape=jax.ShapeDtypeStruct((rows, D_A), BF16),
        compiler_params=_params("arbitrary"),
        name="a_mix",
    )(u, vn, z, wmat, bst)


def _rank_select(g, valid, index, axis, count):
    gm = jnp.where(valid, g, NEG)
    n = g.shape[axis]
    rank = jnp.zeros(g.shape, F32)
    for m in range(n):
        ref = gm[m:m + 1, :] if axis == 0 else gm[:, m:m + 1]
        beats = jnp.where(ref > gm, 1.0, jnp.where(ref == gm, jnp.where(m < index, 1.0, 0.0), 0.0))
        rank = rank + beats
    return jnp.where(valid, jnp.where(rank < float(count), 1.0, 0.0), 0.0), rank


POS_SPLIT = 16


def _moba_prompt_kernel(slope_ref, q_ref, k_ref, v_ref, z_ref, o_ref, qa_scr, ka_scr, va_scr, s_scr, p_scr,
                        *, seq):
    nb = seq // MOBA_BLOCK
    assert nb + 2 <= LANES and seq // POS_SPLIT <= 256
    slope = slope_ref[pl.program_id(1)]

    @pl.when((pl.program_id(0) == 0) & (pl.program_id(1) == 0))
    def _():
        pos = lax.broadcasted_iota(jnp.int32, (seq, LANES), 0)
        lane = lax.broadcasted_iota(jnp.int32, (seq, LANES), 1)
        onehot = jnp.where(lane == _div_pow2(pos, MOBA_BLOCK), 1.0, 0.0)
        hi = _div_pow2(pos, POS_SPLIT).astype(F32)
        lo = (pos - _div_pow2(pos, POS_SPLIT) * POS_SPLIT).astype(F32)
        aug = jnp.where(lane < nb, onehot, jnp.where(lane == nb, hi, jnp.where(lane == nb + 1, lo, 0.0)))
        ka_scr[:, B_HEAD_DIM:] = aug.astype(BF16)
        va_scr[:, B_HEAD_DIM:] = jnp.ones((seq, LANES), BF16)

    k = k_ref[0]
    ka_scr[:, 0:B_HEAD_DIM] = k.astype(BF16)
    va_scr[:, 0:B_HEAD_DIM] = v_ref[0].astype(BF16)

    kbar = jnp.mean(k.reshape(nb, MOBA_BLOCK, B_HEAD_DIM), axis=1)
    qf = q_ref[0] * QK_SCALE
    gt = lax.dot_general(kbar, qf, NT_DIMS, precision=HIGHEST, preferred_element_type=F32)
    blk = lax.broadcasted_iota(jnp.int32, (nb, seq), 0)
    own = _div_pow2(lax.broadcasted_iota(jnp.int32, (nb, seq), 1), MOBA_BLOCK)
    sel, _ = _rank_select(gt, blk < own, blk, 0, MOBA_TOPK)
    pen = jnp.where(sel > 0.5, 0.0, jnp.where(blk == own, 0.0, NEG))
    pen = jnp.concatenate([pen, jnp.zeros((LANES - nb, seq), F32)], axis=0)
    row = lax.broadcasted_iota(jnp.int32, (LANES, seq), 0)
    aug_t = jnp.where(row == nb, POS_SPLIT * slope, jnp.where(row == nb + 1, slope, pen))
    qa_scr[:, B_HEAD_DIM:] = aug_t.T.astype(BF16)
    qa_scr[:, 0:B_HEAD_DIM] = qf.astype(BF16)

    causal = (lax.broadcasted_iota(jnp.int32, (MOBA_BLOCK, MOBA_BLOCK), 0)
              >= lax.broadcasted_iota(jnp.int32, (MOBA_BLOCK, MOBA_BLOCK), 1))
    offs = [MOBA_BLOCK * (jb * (jb + 1) // 2) for jb in range(nb + 1)]
    for jb in range(nb):
        rows = slice(jb * MOBA_BLOCK, (jb + 1) * MOBA_BLOCK)
        nk = (jb + 1) * MOBA_BLOCK
        s_scr[:, offs[jb]:offs[jb + 1]] = lax.dot_general(
            qa_scr[rows, :], ka_scr[0:nk, :], NT_DIMS, preferred_element_type=F32)
    for jb in range(nb):
        n_past = jb * MOBA_BLOCK
        own_cols = slice(offs[jb] + n_past, offs[jb + 1])
        own_s = jnp.where(causal, s_scr[:, own_cols], NEG)
        m = jnp.max(own_s, axis=-1, keepdims=True)
        if jb > 0:
            past_cols = slice(offs[jb], offs[jb] + n_past)
            m = jnp.maximum(m, jnp.max(s_scr[:, past_cols], axis=-1, keepdims=True))
            p_scr[:, past_cols] = jnp.exp(s_scr[:, past_cols] - m).astype(BF16)
        p_scr[:, own_cols] = jnp.exp(own_s - m).astype(BF16)
    for jb in range(nb):
        rows = slice(jb * MOBA_BLOCK, (jb + 1) * MOBA_BLOCK)
        nk = (jb + 1) * MOBA_BLOCK
        o = jnp.dot(p_scr[:, offs[jb]:offs[jb + 1]], va_scr[0:nk, :], preferred_element_type=F32)
        y = o[:, 0:B_HEAD_DIM] / o[:, B_HEAD_DIM:B_HEAD_DIM + 1]
        o_ref[0, rows, :] = (y * z_ref[0, rows, :]).astype(o_ref.dtype)


def _alibi_slopes():
    return jnp.asarray(np.array([2.0 ** (-8.0 * (h + 1) / B_HEADS) for h in range(B_HEADS)], np.float32))


def _moba_prompt(q, k, v, zs, batch, seq):
    shp = (batch, seq, D_B)
    spec = pl.BlockSpec((1, seq, B_HEAD_DIM), lambda b, h: (b, 0, h))
    nb = seq // MOBA_BLOCK
    n_score_cols = MOBA_BLOCK * (nb * (nb + 1) // 2)
    out = pl.pallas_call(
        functools.partial(_moba_prompt_kernel, seq=seq),
        grid=(batch, B_HEADS),
        in_specs=[pl.BlockSpec(memory_space=pltpu.SMEM), spec, spec, spec, spec],
        out_specs=spec,
        out_shape=jax.ShapeDtypeStruct(shp, BF16),
        scratch_shapes=[pltpu.VMEM((seq, B_HEAD_DIM + LANES), BF16)] * 3
                       + [pltpu.VMEM((MOBA_BLOCK, n_score_cols), F32),
                          pltpu.VMEM((MOBA_BLOCK, n_score_cols), BF16)],
        compiler_params=_params("arbitrary", "arbitrary"),
        name="moba_prompt",
    )(_alibi_slopes(), q.reshape(shp), k.reshape(shp), v.reshape(shp), zs.reshape(shp))
    return out.reshape(batch * seq, D_B)


CHUNK_PAGES = 8
K_RING = 4


def _moba_sample_kernel(pt_ref, slope_ref, trow_ref, q_ref, kn_ref, vn_ref, z_ref, ck_ref, cv_ref, o_ref,
                        kbuf, vbuf, ksem, vsem, isem, idx_v, idx_s,
                        qbd_scr, qbf_scr, cat_scr, s_scr, p_scr, ksum_scr, onew_scr, l_scr,
                        *, n_seq, n_new, n_pages):
    pg = CHUNK_PAGES
    nch = n_pages // pg
    total = n_seq * nch
    n_blocks = n_pages * PAGE_SIZE // MOBA_BLOCK
    blocks_per_chunk = pg * PAGE_SIZE // MOBA_BLOCK
    pages_per_block = MOBA_BLOCK // PAGE_SIZE
    rows = n_new * B_HEADS
    t = pl.program_id(0)
    head_of_lane = _div_pow2(lax.broadcasted_iota(jnp.int32, (B_HEADS, D_B), 1), B_HEAD_DIM)
    diag = head_of_lane == lax.broadcasted_iota(jnp.int32, (B_HEADS, D_B), 0)
    slope = slope_ref[:, 0:1]
    trow = trow_ref[:, 0:1]

    def k_copy(page, slot, r):
        return pltpu.make_async_copy(ck_ref.at[page], kbuf.at[slot, r], ksem.at[slot])

    def start_chunk(seq, ch, slot):
        for r in range(pg):
            k_copy(pt_ref[seq, ch * pg + r], slot, r).start()

    def v_copy(page, r, j, half):
        dst = vbuf.at[r, pl.ds((j * pages_per_block + half) * PAGE_SIZE, PAGE_SIZE), :]
        return pltpu.make_async_copy(cv_ref.at[page, :, r % B_HEADS, :], dst, vsem.at[0])

    @pl.when(t == 0)
    def _():
        for f0 in range(K_RING - 1):
            start_chunk(0, f0, f0)

    @pl.when(t < n_seq)
    def _():
        tiles = []
        for qi in range(n_new):
            qrow = q_ref[0, qi:qi + 1, :] * QK_SCALE
            tiles.append(jnp.where(diag, jnp.broadcast_to(qrow, (B_HEADS, D_B)), 0.0))
        qbd = jnp.concatenate(tiles, axis=0)
        qbf_scr[...] = qbd
        qbd_scr[...] = qbd.astype(BF16)

        def chunk_body(c, carry):
            f = t * nch + c
            ahead = f + (K_RING - 1)

            @pl.when(ahead < total)
            def _():
                start_chunk(lax.div(ahead, jnp.int32(nch)), lax.rem(ahead, jnp.int32(nch)),
                            lax.rem(ahead, jnp.int32(K_RING)))

            slot = lax.rem(f, jnp.int32(K_RING))
            for r in range(pg):
                k_copy(0, slot, r).wait()
            pair = None
            for r in range(pg):
                for hd in range(B_HEADS):
                    part = kbuf[slot, r, pl.ds(hd, PAGE_SIZE, stride=B_HEADS), :]
                    cat_scr[r * PAGE_SIZE:(r + 1) * PAGE_SIZE,
                            hd * B_HEAD_DIM:(hd + 1) * B_HEAD_DIM] = part.astype(BF16)
                ps = jnp.sum(kbuf[slot, r].reshape(PAGE_SIZE, B_HEADS, B_HEAD_DIM), axis=0)
                if r % pages_per_block == 0:
                    pair = ps
                else:
                    pair = pair + ps
                if r % pages_per_block == pages_per_block - 1:
                    ksum_scr[c * blocks_per_chunk + r // pages_per_block] = pair
            s_scr[c] = lax.dot_general(qbd_scr[...], cat_scr[...], NT_DIMS, preferred_element_type=F32)
            return carry

        lax.fori_loop(0, nch, chunk_body, 0)

    @pl.when(t >= 1)
    def _():
        for r in range(rows):
            for j in range(MOBA_TOPK):
                for half in range(pages_per_block):
                    v_copy(0, r, j, half).wait()
        head_rows = []
        for r in range(rows):
            parts = [p_scr[idx_s[r, j], r:r + 1, :] for j in range(MOBA_TOPK)]
            p_r = jnp.broadcast_to(jnp.concatenate(parts, axis=1), (SUBLANES, MOBA_TOPK * MOBA_BLOCK))
            o = jnp.dot(p_r.astype(BF16), vbuf[r].astype(BF16), preferred_element_type=F32)
            head_rows.append(o[0:1, :])
        out_rows = []
        for qi in range(n_new):
            sl = slice(qi * B_HEADS, (qi + 1) * B_HEADS)
            o_sel = jnp.concatenate(head_rows[sl], axis=1)
            o_new = jnp.sum(jnp.where(diag, onew_scr[sl, :], 0.0), axis=0, keepdims=True)
            l_row = jnp.concatenate([l_scr[r:r + 1, :] for r in range(sl.start, sl.stop)], axis=1)
            out_rows.append((o_sel + o_new) / l_row)
        o_ref[0] = jnp.concatenate(out_rows, axis=0) * z_ref[0]

    @pl.when(t < n_seq)
    def _():
        kbar = jnp.concatenate([ksum_scr[:, hd, :] for hd in range(B_HEADS)], axis=1)
        kbar = kbar * (1.0 / MOBA_BLOCK)
        g = lax.dot_general(qbf_scr[...], kbar, NT_DIMS, precision=HIGHEST,
                            preferred_element_type=F32)
        blk = lax.broadcasted_iota(jnp.int32, (rows, n_blocks), 1)
        sel, rank = _rank_select(g, blk >= 0, blk, 1, MOBA_TOPK)

        lane = lax.broadcasted_iota(jnp.int32, (rows, LANES), 1)
        ids = jnp.zeros((rows, LANES), F32)
        for j in range(MOBA_TOPK):
            col = jnp.sum(jnp.where(rank == float(j), blk.astype(F32), 0.0), axis=-1, keepdims=True)
            ids = jnp.where(lane == j, col, ids)
        idx_v[...] = ids.astype(jnp.int32)
        to_smem = pltpu.make_async_copy(idx_v, idx_s, isem.at[0])
        to_smem.start()
        to_smem.wait()
        for r in range(rows):
            for j in range(MOBA_TOPK):
                for half in range(pages_per_block):
                    v_copy(pt_ref[t, idx_s[r, j] * pages_per_block + half], r, j, half).start()

        key = lax.broadcasted_iota(jnp.int32, (rows, MOBA_BLOCK), 1).astype(F32)
        top = None
        for n in range(n_blocks):
            gi, a = divmod(n, blocks_per_chunk)
            dist = trow - (float(n * MOBA_BLOCK) + key)
            slab = s_scr[gi, :, a * MOBA_BLOCK:(a + 1) * MOBA_BLOCK] - slope * dist
            slab = jnp.where(sel[:, n:n + 1] > 0.5, slab, NEG)
            p_scr[n] = slab
            top = slab if top is None else jnp.maximum(top, slab)
        m = jnp.max(top, axis=-1, keepdims=True)
        knew = jnp.concatenate([kn_ref[0], jnp.zeros((LANES - n_new, D_B), F32)], axis=0).astype(BF16)
        s_new = lax.dot_general(qbd_scr[...], knew, NT_DIMS, preferred_element_type=F32)
        dist_new = trow - (float(n_pages * PAGE_SIZE) + lane.astype(F32))
        s_new = jnp.where(dist_new >= 0.0, s_new - slope * dist_new, NEG)
        m = jnp.maximum(m, jnp.max(s_new, axis=-1, keepdims=True))
        tot = jnp.zeros((rows, MOBA_BLOCK), F32)
        for n in range(n_blocks):
            p = jnp.exp(p_scr[n] - m)
            tot = tot + p
            p_scr[n] = p
        pn = jnp.exp(s_new - m)
        l = jnp.sum(tot, axis=-1, keepdims=True) + jnp.sum(pn, axis=-1, keepdims=True)
        l_scr[...] = jnp.broadcast_to(l, (rows, LANES))
        vnew = jnp.concatenate([vn_ref[0], jnp.zeros((LANES - n_new, D_B), F32)], axis=0).astype(BF16)
        onew_scr[...] = jnp.dot(pn.astype(BF16), vnew, preferred_element_type=F32)


def _moba_sample(q, kn, vn, zs, cache_k, cache_v, page_table, n_seq, n_new):
    n_pages = page_table.shape[1]
    n_phys = cache_k.shape[0]
    pg = CHUNK_PAGES
    nch = n_pages // pg
    assert n_pages % pg == 0 and K_RING - 1 <= nch
    rows = n_new * B_HEADS
    past = n_pages * PAGE_SIZE
    shp = (n_seq, n_new, D_B)
    slope_rows = np.tile(np.array([2.0 ** (-8.0 * (h + 1) / B_HEADS) for h in range(B_HEADS)], np.float32),
                         n_new)
    t_rows = np.repeat(past + np.arange(n_new, dtype=np.float32), B_HEADS)
    slope_rows = jnp.asarray(np.tile(slope_rows[:, None], (1, LANES)))
    t_rows = jnp.asarray(np.tile(t_rows[:, None], (1, LANES)))

    const = pl.BlockSpec((rows, LANES), lambda t, pt: (0, 0))
    cur = pl.BlockSpec((1, n_new, D_B), lambda t, pt: (jnp.minimum(t, n_seq - 1), 0, 0))
    prev = pl.BlockSpec((1, n_new, D_B), lambda t, pt: (jnp.maximum(t - 1, 0), 0, 0))
    hbm = pl.BlockSpec(memory_space=pl.ANY)

    page_rows = PAGE_SIZE * B_HEADS
    n_blocks = past // MOBA_BLOCK
    out = pl.pallas_call(
        functools.partial(_moba_sample_kernel, n_seq=n_seq, n_new=n_new, n_pages=n_pages),
        grid_spec=pltpu.PrefetchScalarGridSpec(
            num_scalar_prefetch=1,
            grid=(n_seq + 1,),
            in_specs=[const, const, cur, cur, cur, prev, hbm, hbm],
            out_specs=prev,
            scratch_shapes=[
                pltpu.VMEM((K_RING, pg, page_rows, B_HEAD_DIM), F32),
                pltpu.VMEM((rows, MOBA_TOPK * MOBA_BLOCK, B_HEAD_DIM), F32),
                pltpu.SemaphoreType.DMA((K_RING,)),
                pltpu.SemaphoreType.DMA((1,)),
                pltpu.SemaphoreType.DMA((1,)),
                pltpu.VMEM((rows, LANES), jnp.int32),
                pltpu.SMEM((rows, LANES), jnp.int32),
                pltpu.VMEM((rows, D_B), BF16),
                pltpu.VMEM((rows, D_B), F32),
                pltpu.VMEM((pg * PAGE_SIZE, D_B), BF16),
                pltpu.VMEM((nch, rows, pg * PAGE_SIZE), F32),
                pltpu.VMEM((n_blocks, rows, MOBA_BLOCK), F32),
                pltpu.VMEM((n_blocks, B_HEADS, B_HEAD_DIM), F32),
                pltpu.VMEM((rows, D_B), F32),
                pltpu.VMEM((rows, LANES), F32),
            ]),
        out_shape=jax.ShapeDtypeStruct(shp, F32),
        compiler_params=_params("arbitrary"),
        name="moba_sample",
    )(page_table, slope_rows, t_rows, q.reshape(shp), kn.reshape(shp), vn.reshape(shp), zs.reshape(shp),
      cache_k.reshape(n_phys, page_rows, B_HEAD_DIM), cache_v)
    return out.reshape(n_seq * n_new, D_B)


def _out_proj_kernel(ya_ref, yb_ref, w_ref, x_ref, gt_ref, g_ref, *rest, final):
    wbf_ref = rest[-1]

    @pl.when(pl.program_id(0) == 0)
    def _():
        wbf_ref[...] = w_ref[...].astype(BF16)

    tm, half = ya_ref.shape
    sub = min(tm, OUT_SUB_ROWS)
    mod_rows = _mod_rows
    for c in range(tm // sub):
        rs = slice(c * sub, (c + 1) * sub)
        acc = jnp.dot(ya_ref[rs, :].astype(BF16), wbf_ref[0:half, :], preferred_element_type=F32)
        acc = acc + jnp.dot(yb_ref[rs, :].astype(BF16), wbf_ref[half:, :], preferred_element_type=F32)
        x = x_ref[rs, :] + mod_rows(gt_ref, rs) * acc
        r = lax.rsqrt(jnp.mean(x * x, axis=-1, keepdims=True) + EPS)
        if final:
            o_ref = rest[0]
            o_ref[rs, :] = (x * r) * g_ref[...]
        else:
            sc_ref, sh_ref, x_out_ref, h_ref = rest[:4]
            x_out_ref[rs, :] = x
            h = (x * r) * g_ref[...] * (1.0 + mod_rows(sc_ref, rs)) + mod_rows(sh_ref, rs)
            h_ref[rs, :] = h.astype(h_ref.dtype)


def _out_proj(ya, yb, ya_col, yb_col, w, x, mod, g, next_mod, tm):
    rows = x.shape[0]
    half = D_MODEL // 2
    final = next_mod is None
    row_f32 = pl.BlockSpec((tm, D_MODEL), lambda i: (i, 0))
    in_specs = [
        pl.BlockSpec((tm, half), lambda i: (i, ya_col)),
        pl.BlockSpec((tm, half), lambda i: (i, yb_col)),
        pl.BlockSpec((D_MODEL, D_MODEL), lambda i: (0, 0), pipeline_mode=pl.Buffered(1)),
        row_f32,
        mod.spec(2, tm),
        pl.BlockSpec((1, D_MODEL), lambda i: (0, 0)),
    ]
    args = [ya, yb, w, x, mod.arr, g.reshape(1, D_MODEL)]
    if final:
        out_specs, out_shape = row_f32, jax.ShapeDtypeStruct((rows, D_MODEL), F32)
    else:
        in_specs += [next_mod.spec(1, tm), next_mod.spec(0, tm)]
        args += [next_mod.arr, next_mod.arr]
        out_specs = [row_f32, row_f32]
        out_shape = [jax.ShapeDtypeStruct((rows, D_MODEL), F32), jax.ShapeDtypeStruct((rows, D_MODEL), BF16)]
    return pl.pallas_call(
        functools.partial(_out_proj_kernel, final=final),
        grid=(rows // tm,),
        in_specs=in_specs,
        out_specs=out_specs,
        out_shape=out_shape,
        scratch_shapes=[pltpu.VMEM((D_MODEL, D_MODEL), BF16)],
        compiler_params=_params("arbitrary"),
        name="out_proj_final" if final else "out_proj",
    )(*args)


def _conv_proj_kernel(h_ref, wa_ref, wb_ref, wz_ref, glu_ref, sz_ref, wbf_ref):
    @pl.when(pl.program_id(1) == 0)
    def _():
        wbf_ref[0] = wa_ref[...].astype(BF16)
        wbf_ref[1] = wb_ref[...].astype(BF16)
        wbf_ref[2] = wz_ref[...].astype(BF16)

    h = h_ref[...]
    a = jnp.dot(h, wbf_ref[0], preferred_element_type=F32)
    b = jnp.dot(h, wbf_ref[1], preferred_element_type=F32)
    glu_ref[...] = a * _sigmoid(b)
    sz_ref[...] = _silu(jnp.dot(h, wbf_ref[2], preferred_element_type=F32))


def _conv_proj(h, w, tm):
    rows = h.shape[0]
    tn = 512
    nj = D_C // tn
    out = pl.BlockSpec((tm, tn), lambda j, i: (i, j))
    return pl.pallas_call(
        _conv_proj_kernel,
        grid=(nj, rows // tm),
        in_specs=[
            pl.BlockSpec((tm, D_MODEL), lambda j, i: (i, 0)),
            pl.BlockSpec((D_MODEL, tn), lambda j, i: (0, j)),
            pl.BlockSpec((D_MODEL, tn), lambda j, i: (0, nj + j)),
            pl.BlockSpec((D_MODEL, tn), lambda j, i: (0, 2 * nj + j)),
        ],
        out_specs=[out, out],
        out_shape=[jax.ShapeDtypeStruct((rows, D_C), F32)] * 2,
        scratch_shapes=[pltpu.VMEM((3, D_MODEL, tn), BF16)],
        compiler_params=_params("arbitrary", "arbitrary"),
        name="conv_proj",
    )(h, w, w, w)


HALO = 32
CONV_LANES = 128
LN_ROWS = 16


def _conv_ln_kernel(glu_ref, halo_ref, sz_ref, w_ref, cb_ref, g_ref, b_ref, o_ref, buf_scr, y_scr, wb_scr,
                    *, tiles_per_seq, tm):
    first = (pl.program_id(0) % tiles_per_seq) == 0

    @pl.when(pl.program_id(0) == 0)
    def _():
        for k in range(CONV_W):
            wb_scr[k] = jnp.broadcast_to(w_ref[k:k + 1, :], (SUBLANES, D_C))

    @pl.when(first)
    def _():
        buf_scr[0:HALO, :] = jnp.zeros((HALO, D_C), F32)

    @pl.when(jnp.logical_not(first))
    def _():
        buf_scr[0:HALO, :] = halo_ref[...]

    buf_scr[HALO:HALO + tm, :] = glu_ref[...]

    off = HALO - (CONV_W - 1)
    taps = [[(a, SUBLANES * a + r - off) for a in range((CONV_W + off) // SUBLANES + 1)
             if 0 <= SUBLANES * a + r - off < CONV_W] for r in range(SUBLANES)]
    rowid = lax.broadcasted_iota(jnp.int32, (SUBLANES, CONV_LANES), 0)

    for lc in range(D_C // CONV_LANES):
        lanes = slice(lc * CONV_LANES, (lc + 1) * CONV_LANES)

        def z_block(u0, r):
            acc = None
            for a, k in taps[r]:
                term = wb_scr[k, :, lanes] * buf_scr[pl.ds(u0 + SUBLANES * a, SUBLANES), lanes]
                acc = term if acc is None else acc + term
            return acc

        def body(blk, carry):
            t0 = pl.multiple_of(blk * SUBLANES, SUBLANES)
            terms = [z_block(t0, 0)]
            nxt = []
            for r in range(1, SUBLANES):
                zn = z_block(t0 + SUBLANES, r)
                mixed = jnp.where(rowid >= r, carry[r - 1], zn)
                terms.append(pltpu.roll(mixed, SUBLANES - r, 0))
                nxt.append(zn)
            while len(terms) > 1:
                terms = [terms[i] + terms[i + 1] for i in range(0, len(terms), 2)]
            y_scr[pl.ds(t0, SUBLANES), lanes] = terms[0]
            return tuple(nxt)

        init = tuple(z_block(0, r) for r in range(1, SUBLANES))
        lax.fori_loop(0, tm // SUBLANES, body, init, unroll=4)

    def norm_rows(blk, carry):
        r0 = pl.multiple_of(blk * LN_ROWS, LN_ROWS)
        rs = pl.ds(r0, LN_ROWS)
        y = _layernorm_rows(y_scr[rs, :] + cb_ref[...], g_ref[...], b_ref[...])
        o_ref[rs, :] = (_silu(y) * sz_ref[rs, :]).astype(o_ref.dtype)
        return carry

    lax.fori_loop(0, tm // LN_ROWS, norm_rows, 0, unroll=8)


def _conv_ln(glu, sz, conv_w, conv_b, ln_g, ln_b, seq, tm):
    rows = glu.shape[0]
    vec = pl.BlockSpec((1, D_C), lambda i: (0, 0))
    tile = pl.BlockSpec((tm, D_C), lambda i: (i, 0))
    return pl.pallas_call(
        functools.partial(_conv_ln_kernel, tiles_per_seq=seq // tm, tm=tm),
        grid=(rows // tm,),
        in_specs=[tile,
                  pl.BlockSpec((HALO, D_C), lambda i: (jnp.maximum(i * (tm // HALO) - 1, 0), 0)),
                  tile,
                  pl.BlockSpec((CONV_W, D_C), lambda i: (0, 0)),
                  vec, vec, vec],
        out_specs=tile,
        out_shape=jax.ShapeDtypeStruct((rows, D_C), BF16),
        scratch_shapes=[pltpu.VMEM((HALO + tm, D_C), F32), pltpu.VMEM((tm, D_C), F32),
                        pltpu.VMEM((CONV_W, SUBLANES, D_C), F32)],
        compiler_params=_params("arbitrary"),
        name="conv_ln",
    )(glu, glu, sz, conv_w, conv_b.reshape(1, D_C), ln_g.reshape(1, D_C), ln_b.reshape(1, D_C))


def _conv_ln_sample_kernel(st_ref, glu_ref, sz_ref, w_ref, cb_ref, g_ref, b_ref, o_ref, ns_ref, buf_scr,
                           *, n_new):
    hist = CONV_W - 1
    buf_scr[0:hist, :] = st_ref[0]
    buf_scr[hist:hist + n_new, :] = glu_ref[0]
    acc = jnp.zeros((n_new, D_C), F32) + cb_ref[...]
    for k in range(CONV_W):
        acc = acc + w_ref[k:k + 1, :] * buf_scr[k:k + n_new, :]
    y = _layernorm_rows(acc, g_ref[...], b_ref[...])
    o_ref[0] = _silu(y) * sz_ref[0]
    ns_ref[0] = buf_scr[n_new:n_new + hist, :]


def _conv_ln_sample(state, glu, sz, conv_w, conv_b, ln_g, ln_b, n_seq, n_new):
    hist = CONV_W - 1
    vec = pl.BlockSpec((1, D_C), lambda s: (0, 0))
    new = pl.BlockSpec((1, n_new, D_C), lambda s: (s, 0, 0))
    st = pl.BlockSpec((1, hist, D_C), lambda s: (s, 0, 0))
    y, ns = pl.pallas_call(
        functools.partial(_conv_ln_sample_kernel, n_new=n_new),
        grid=(n_seq,),
        in_specs=[st, new, new, pl.BlockSpec((CONV_W, D_C), lambda s: (0, 0)), vec, vec, vec],
        out_specs=[new, st],
        out_shape=[jax.ShapeDtypeStruct((n_seq, n_new, D_C), F32),
                   jax.ShapeDtypeStruct((n_seq, hist, D_C), F32)],
        scratch_shapes=[pltpu.VMEM((hist + n_new, D_C), F32)],
        compiler_params=_params("arbitrary"),
        name="conv_ln_sample",
    )(state, glu.reshape(n_seq, n_new, D_C), sz.reshape(n_seq, n_new, D_C), conv_w,
      conv_b.reshape(1, D_C), ln_g.reshape(1, D_C), ln_b.reshape(1, D_C))
    return y.reshape(n_seq * n_new, D_C), ns


def _trunk(x, mods, tm, seq, weights, attend, conv):
    (norm_g, ab_w_in, a_ln_g, a_ln_b, a_ws, a_bs, ab_w_out, c_w_in, c_w_out, final_norm_g) = weights
    tm_row = min(tm, ROW_TILE)
    u, h = _proj_norm(x, norm_g[0], mods[0], ab_w_in[0], tm)
    seg = lambda s, ep: _proj_seg(h, ab_w_in[0], s, ep, a_ln_g[0], a_ln_b[0], tm)
    vn, za = seg(1, "gelu_ln"), seg(2, "silu")
    q, zb = seg(3, "none"), seg(6, "silu")
    (k, k_heads), (v, v_heads) = seg(4, "heads"), seg(5, "heads")
    ya = _a_mix(u, vn, za, a_ws[0], a_bs[0], seq, tm_row)
    yb = attend(q, k, v, zb)
    x1, h1 = _out_proj(ya, yb, 0, 0, ab_w_out[0], x, mods[0], norm_g[1], mods[1], tm_row)
    glu, sz = _conv_proj(h1, c_w_in[0], tm)
    y2, conv_state = conv(glu, sz)
    y = _out_proj(y2, y2, 0, 1, c_w_out[0], x1, mods[1], final_norm_g, None, tm_row)
    return y, k_heads, v_heads, vn, conv_state


def kernel(x_prompt, x_sample, cache_b_k, cache_b_v, state_c_conv, page_table, c_prompt, c_sample,
           ada_w, ada_b, norm_g, ab_w_in, a_ln_g, a_ln_b, a_ws, a_bs, ab_w_out,
           c_w_in, c_conv_w, c_conv_b, c_ln_g, c_ln_b, c_w_out, final_norm_g):
    batch, seq, _ = x_prompt.shape
    n_seq, n_new, _ = x_sample.shape
    weights = (norm_g, ab_w_in, a_ln_g, a_ln_b, a_ws, a_bs, ab_w_out, c_w_in, c_w_out, final_norm_g)

    mod = _adaln(jnp.concatenate([c_prompt, c_sample], axis=0), ada_w, ada_b)
    mod = mod.reshape(DEPTH, batch + n_seq, 3, D_MODEL)
    tm_p = PROJ_TILE
    mods_p, mods_s = [], []
    for l in range(DEPTH):
        mods_p.append(_Mod(mod[l, :batch].reshape(batch * 3, 1, D_MODEL), seq))
        per_row = jnp.repeat(mod[l, batch:], n_new, axis=0).transpose(1, 0, 2)
        mods_s.append(_Mod(per_row))

    yp, kp, vp, _, glu_p = _trunk(
        x_prompt.reshape(batch * seq, D_MODEL), mods_p, tm_p, seq, weights,
        attend=lambda q, k, v, zb: _moba_prompt(q, k, v, zb, batch, seq),
        conv=lambda glu, sz: (_conv_ln(glu, sz, c_conv_w[0], c_conv_b[0], c_ln_g[0], c_ln_b[0], seq, ROW_TILE), glu))
    ys, ks, vs, vas, conv_s = _trunk(
        x_sample.reshape(n_seq * n_new, D_MODEL), mods_s, n_seq * n_new, n_new, weights,
        attend=lambda q, k, v, zb: _moba_sample(q, k, v, zb, cache_b_k[0], cache_b_v[0], page_table,
                                               n_seq, n_new),
        conv=lambda glu, sz: _conv_ln_sample(state_c_conv[0], glu, sz, c_conv_w[0], c_conv_b[0],
                                             c_ln_g[0], c_ln_b[0], n_seq, n_new))

    hd = (B_HEADS, B_HEAD_DIM)
    conv_p = glu_p.reshape(batch, seq, D_C)[:, seq - (CONV_W - 1):, :]
    return (yp.reshape(batch, seq, D_MODEL), ys.reshape(n_seq, n_new, D_MODEL),
            kp.reshape(1, batch, seq, *hd), vp.reshape(1, batch, seq, *hd),
            ks.reshape(1, n_seq, n_new, *hd), vs.reshape(1, n_seq, n_new, *hd),
            vas.reshape(1, n_seq, n_new, D_A), conv_p[None], conv_s[None])
```

```python
import functools

import numpy as np
import jax
import jax.numpy as jnp
from jax import lax
from jax.experimental import pallas as pl
from jax.experimental.pallas import tpu as pltpu

F32 = jnp.float32
BF16 = jnp.bfloat16
HIGHEST = lax.Precision.HIGHEST

D_MODEL = 2048
DEPTH = 2
PAST_LEN = 8192
PAGE_SIZE = 128
D_A = 1024
A_GROUPS = 8
A_CHUNK = 128
B_HEADS = 8
B_HEAD_DIM = 128
D_B = B_HEADS * B_HEAD_DIM
MOBA_BLOCK = 256
MOBA_TOPK = 3
D_C = D_MODEL
CONV_W = 31
EPS = 1e-6
NEG = -1e30
QK_SCALE = B_HEAD_DIM ** -0.5

VMEM_LIMIT_BYTES = 56 * 1024 * 1024
SUBLANES = 8
LANES = 128
PROJ_TILE = 1024
ROW_TILE = 512
OUT_SUB_ROWS = 256

NT_DIMS = (((1,), (1,)), ((), ()))


def _params(*sem):
    return pltpu.CompilerParams(dimension_semantics=sem, vmem_limit_bytes=VMEM_LIMIT_BYTES)


def _div_pow2(x, d):
    shift = int(d).bit_length() - 1
    assert 1 << shift == d
    return lax.shift_right_logical(x, jnp.int32(shift))


def _sigmoid(x):
    return jax.nn.sigmoid(x)


def _silu(x):
    return x * _sigmoid(x)


def _gelu_tanh(x):
    c = np.float32(np.sqrt(2.0 / np.pi))
    return x * (0.5 * (1.0 + jnp.tanh(c * (x + 0.044715 * (x * x * x)))))


def _layernorm_rows(x, g, b):
    mu = jnp.mean(x, axis=-1, keepdims=True)
    xc = x - mu
    var = jnp.mean(xc * xc, axis=-1, keepdims=True)
    return (xc * lax.rsqrt(var + EPS)) * g + b


def _adaln_kernel(c_ref, w_ref, b_ref, o_ref):
    s = _silu(c_ref[...]).astype(BF16)
    o_ref[0] = jnp.dot(s, w_ref[0].astype(BF16), preferred_element_type=F32) + b_ref[0]


def _adaln(c_all, ada_w, ada_b):
    n = c_all.shape[0]
    tn = 1024
    return pl.pallas_call(
        _adaln_kernel,
        grid=(DEPTH, 3 * D_MODEL // tn),
        in_specs=[
            pl.BlockSpec((n, D_MODEL), lambda l, j: (0, 0)),
            pl.BlockSpec((1, D_MODEL, tn), lambda l, j: (l, 0, j)),
            pl.BlockSpec((1, 1, tn), lambda l, j: (l, 0, j)),
        ],
        out_specs=pl.BlockSpec((1, n, tn), lambda l, j: (l, 0, j)),
        out_shape=jax.ShapeDtypeStruct((DEPTH, n, 3 * D_MODEL), F32),
        compiler_params=_params("arbitrary", "arbitrary"),
        name="adaln",
    )(c_all, ada_w, ada_b.reshape(DEPTH, 1, 3 * D_MODEL))


class _Mod:
    def __init__(self, arr, seq_len=None):
        self.arr = arr
        self.rows = arr.shape[1]
        self.seq_len = seq_len

    def spec(self, which, tm):
        if self.seq_len is None:
            return pl.BlockSpec((1, self.rows, D_MODEL), lambda i: (which, 0, 0))
        assert self.seq_len % tm == 0
        tiles = self.seq_len // tm
        return pl.BlockSpec((1, 1, D_MODEL), lambda i: ((i // tiles) * 3 + which, 0, 0))


def _mod_rows(ref, rs):
    return ref[0] if ref.shape[1] == 1 else ref[0, rs, :]


def _proj_norm_kernel(x_ref, ng_ref, sc_ref, sh_ref, w_ref, u_ref, h_ref, wbf_ref):
    @pl.when(pl.program_id(0) == 0)
    def _():
        wbf_ref[...] = w_ref[...].astype(BF16)

    tm = x_ref.shape[0]
    sub = min(tm, OUT_SUB_ROWS)
    for c in range(tm // sub):
        rs = slice(c * sub, (c + 1) * sub)
        x = x_ref[rs, :]
        r = lax.rsqrt(jnp.mean(x * x, axis=-1, keepdims=True) + EPS)
        h = ((x * r) * ng_ref[...] * (1.0 + _mod_rows(sc_ref, rs)) + _mod_rows(sh_ref, rs)).astype(BF16)
        h_ref[rs, :] = h
        u_ref[rs, :] = _gelu_tanh(jnp.dot(h, wbf_ref[...], preferred_element_type=F32))


def _proj_norm(x, g, mod, w, tm):
    rows = x.shape[0]
    tn = 1024
    return pl.pallas_call(
        _proj_norm_kernel,
        grid=(rows // tm,),
        in_specs=[
            pl.BlockSpec((tm, D_MODEL), lambda i: (i, 0)),
            pl.BlockSpec((1, D_MODEL), lambda i: (0, 0)),
            mod.spec(1, tm),
            mod.spec(0, tm),
            pl.BlockSpec((D_MODEL, tn), lambda i: (0, 0), pipeline_mode=pl.Buffered(1)),
        ],
        out_specs=[pl.BlockSpec((tm, tn), lambda i: (i, 0)), pl.BlockSpec((tm, D_MODEL), lambda i: (i, 0))],
        out_shape=[jax.ShapeDtypeStruct((rows, tn), F32), jax.ShapeDtypeStruct((rows, D_MODEL), BF16)],
        scratch_shapes=[pltpu.VMEM((D_MODEL, tn), BF16)],
        compiler_params=_params("arbitrary"),
        name="proj_norm_gelu",
    )(x, g.reshape(1, D_MODEL), mod.arr, mod.arr, w)


def _proj_seg_kernel(h_ref, w_ref, g_ref, b_ref, *rest, epilogue):
    o_ref, wbf_ref = rest[0], rest[-1]

    @pl.when(pl.program_id(0) == 0)
    def _():
        wbf_ref[...] = w_ref[...].astype(BF16)

    acc = jnp.dot(h_ref[...], wbf_ref[...], preferred_element_type=F32)
    if epilogue == "gelu_ln":
        acc = _layernorm_rows(_gelu_tanh(acc), g_ref[...], b_ref[...])
    elif epilogue == "silu":
        acc = _silu(acc)
    o_ref[...] = acc
    if epilogue == "heads":
        oh_ref = rest[1]
        tm = acc.shape[0]
        for hd in range(B_HEADS):
            oh_ref[pl.ds(hd, tm, stride=B_HEADS), :] = acc[:, hd * B_HEAD_DIM:(hd + 1) * B_HEAD_DIM]


def _proj_seg(h, w, seg, epilogue, ln_g, ln_b, tm):
    rows = h.shape[0]
    tn = 1024
    out_specs = [pl.BlockSpec((tm, tn), lambda i: (i, 0))]
    out_shape = [jax.ShapeDtypeStruct((rows, tn), F32)]
    if epilogue == "heads":
        out_specs.append(pl.BlockSpec((tm * B_HEADS, B_HEAD_DIM), lambda i: (i, 0)))
        out_shape.append(jax.ShapeDtypeStruct((rows * B_HEADS, B_HEAD_DIM), F32))
    out = pl.pallas_call(
        functools.partial(_proj_seg_kernel, epilogue=epilogue),
        grid=(rows // tm,),
        in_specs=[
            pl.BlockSpec((tm, D_MODEL), lambda i: (i, 0)),
            pl.BlockSpec((D_MODEL, tn), lambda i: (0, seg), pipeline_mode=pl.Buffered(1)),
            pl.BlockSpec((1, tn), lambda i: (0, 0)),
            pl.BlockSpec((1, tn), lambda i: (0, 0)),
        ],
        out_specs=out_specs,
        out_shape=out_shape,
        scratch_shapes=[pltpu.VMEM((D_MODEL, tn), BF16)],
        compiler_params=_params("arbitrary"),
        name="proj_" + epilogue,
    )(h, w, ln_g.reshape(1, tn), ln_b.reshape(1, tn))
    return out if epilogue == "heads" else out[0]


def _a_mix_kernel(h_ref, wz_ref, u_ref, v_ref, w_ref, bst_ref, o_ref, wbf_ref, z_ref, *, seq_len):
    @pl.when(pl.program_id(0) == 0)
    def _():
        wbf_ref[...] = wz_ref[...].astype(BF16)

    z_ref[...] = _silu(jnp.dot(h_ref[...], wbf_ref[...], preferred_element_type=F32))
    tm = h_ref.shape[0]
    row = lax.broadcasted_iota(jnp.int32, (A_CHUNK, A_CHUNK), 0)
    col = lax.broadcasted_iota(jnp.int32, (A_CHUNK, A_CHUNK), 1)
    keep = jnp.where(col <= row, 1.0, 0.0)
    if seq_len < A_CHUNK:
        keep = keep * jnp.where(_div_pow2(row, seq_len) == _div_pow2(col, seq_len), 1.0, 0.0)
    gw = D_A // A_GROUPS
    for g in range(A_GROUPS):
        wg = (w_ref[g] * keep).astype(BF16)
        bias = bst_ref[:, g:g + 1]
        for c in range(tm // A_CHUNK):
            rs = slice(c * A_CHUNK, (c + 1) * A_CHUNK)
            cs = slice(g * gw, (g + 1) * gw)
            mix = jnp.dot(wg, v_ref[rs, cs].astype(BF16), preferred_element_type=F32) + bias
            o_ref[rs, cs] = ((u_ref[rs, cs] * mix) * z_ref[rs, cs]).astype(o_ref.dtype)


def _a_mix(h, w, seg, u, vn, ws, bs, seq_len, tm):
    rows = u.shape[0]
    n = min(seq_len, A_CHUNK)
    reps = A_CHUNK // n
    wmat = jnp.tile(ws[:, :n, :n], (1, reps, reps))
    bst = jnp.tile(bs[:, :n], (1, reps)).T
    spec = pl.BlockSpec((tm, D_A), lambda i: (i, 0))
    return pl.pallas_call(
        functools.partial(_a_mix_kernel, seq_len=n),
        grid=(rows // tm,),
        in_specs=[pl.BlockSpec((tm, D_MODEL), lambda i: (i, 0)),
                  pl.BlockSpec((D_MODEL, D_A), lambda i: (0, seg), pipeline_mode=pl.Buffered(1)),
                  spec, spec,
                  pl.BlockSpec((A_GROUPS, A_CHUNK, A_CHUNK), lambda i: (0, 0, 0)),
                  pl.BlockSpec((A_CHUNK, A_GROUPS), lambda i: (0, 0))],
        out_specs=spec,
        out_shape=jax.ShapeDtypeStruct((rows, D_A), BF16),
        scratch_shapes=[pltpu.VMEM((D_MODEL, D_A), BF16), pltpu.VMEM((tm, D_A), F32)],
        compiler_params=_params("arbitrary"),
        name="proj_silu_a_mix",
    )(h, w, u, vn, wmat, bst)


def _rank_select(g, valid, index, axis, count):
    gm = jnp.where(valid, g, NEG)
    n = g.shape[axis]
    rank = jnp.zeros(g.shape, F32)
    for m in range(n):
        ref = gm[m:m + 1, :] if axis == 0 else gm[:, m:m + 1]
        beats = jnp.where(ref > gm, 1.0, jnp.where(ref == gm, jnp.where(m < index, 1.0, 0.0), 0.0))
        rank = rank + beats
    return jnp.where(valid, jnp.where(rank < float(count), 1.0, 0.0), 0.0), rank


POS_SPLIT = 16


def _moba_prompt_kernel(slope_ref, q_ref, k_ref, v_ref, z_ref, o_ref, qa_scr, ka_scr, va_scr, s_scr, p_scr,
                        *, seq):
    nb = seq // MOBA_BLOCK
    assert nb + 2 <= LANES and seq // POS_SPLIT <= 256
    slope = slope_ref[pl.program_id(1)]

    @pl.when((pl.program_id(0) == 0) & (pl.program_id(1) == 0))
    def _():
        pos = lax.broadcasted_iota(jnp.int32, (seq, LANES), 0)
        lane = lax.broadcasted_iota(jnp.int32, (seq, LANES), 1)
        onehot = jnp.where(lane == _div_pow2(pos, MOBA_BLOCK), 1.0, 0.0)
        hi = _div_pow2(pos, POS_SPLIT).astype(F32)
        lo = (pos - _div_pow2(pos, POS_SPLIT) * POS_SPLIT).astype(F32)
        aug = jnp.where(lane < nb, onehot, jnp.where(lane == nb, hi, jnp.where(lane == nb + 1, lo, 0.0)))
        ka_scr[:, B_HEAD_DIM:] = aug.astype(BF16)
        va_scr[:, B_HEAD_DIM:] = jnp.ones((seq, LANES), BF16)

    k = k_ref[0]
    ka_scr[:, 0:B_HEAD_DIM] = k.astype(BF16)
    va_scr[:, 0:B_HEAD_DIM] = v_ref[0].astype(BF16)

    kbar = jnp.mean(k.reshape(nb, MOBA_BLOCK, B_HEAD_DIM), axis=1)
    qf = q_ref[0] * QK_SCALE
    gt = lax.dot_general(kbar, qf, NT_DIMS, precision=HIGHEST, preferred_element_type=F32)
    blk = lax.broadcasted_iota(jnp.int32, (nb, seq), 0)
    own = _div_pow2(lax.broadcasted_iota(jnp.int32, (nb, seq), 1), MOBA_BLOCK)
    sel, _ = _rank_select(gt, blk < own, blk, 0, MOBA_TOPK)
    pen = jnp.where(sel > 0.5, 0.0, jnp.where(blk == own, 0.0, NEG))
    pen = jnp.concatenate([pen, jnp.zeros((LANES - nb, seq), F32)], axis=0)
    row = lax.broadcasted_iota(jnp.int32, (LANES, seq), 0)
    aug_t = jnp.where(row == nb, POS_SPLIT * slope, jnp.where(row == nb + 1, slope, pen))
    qa_scr[:, B_HEAD_DIM:] = aug_t.T.astype(BF16)
    qa_scr[:, 0:B_HEAD_DIM] = qf.astype(BF16)

    causal = (lax.broadcasted_iota(jnp.int32, (MOBA_BLOCK, MOBA_BLOCK), 0)
              >= lax.broadcasted_iota(jnp.int32, (MOBA_BLOCK, MOBA_BLOCK), 1))
    offs = [MOBA_BLOCK * (jb * (jb + 1) // 2) for jb in range(nb + 1)]

    def scores(jb):
        rows = slice(jb * MOBA_BLOCK, (jb + 1) * MOBA_BLOCK)
        nk = (jb + 1) * MOBA_BLOCK
        s_scr[:, offs[jb]:offs[jb + 1]] = lax.dot_general(
            qa_scr[rows, :], ka_scr[0:nk, :], NT_DIMS, preferred_element_type=F32)

    def weights(jb):
        n_past = jb * MOBA_BLOCK
        own_cols = slice(offs[jb] + n_past, offs[jb + 1])
        own_s = jnp.where(causal, s_scr[:, own_cols], NEG)
        m = jnp.max(own_s, axis=-1, keepdims=True)
        if jb > 0:
            past_cols = slice(offs[jb], offs[jb] + n_past)
            m = jnp.maximum(m, jnp.max(s_scr[:, past_cols], axis=-1, keepdims=True))
            p_scr[:, past_cols] = jnp.exp(s_scr[:, past_cols] - m).astype(BF16)
        p_scr[:, own_cols] = jnp.exp(own_s - m).astype(BF16)

    def outputs(jb):
        rows = slice(jb * MOBA_BLOCK, (jb + 1) * MOBA_BLOCK)
        nk = (jb + 1) * MOBA_BLOCK
        o = jnp.dot(p_scr[:, offs[jb]:offs[jb + 1]], va_scr[0:nk, :], preferred_element_type=F32)
        y = o[:, 0:B_HEAD_DIM] / o[:, B_HEAD_DIM:B_HEAD_DIM + 1]
        o_ref[0, rows, :] = (y * z_ref[0, rows, :]).astype(o_ref.dtype)

    for jb in range(nb):
        scores(jb)
    for jb in range(nb + 1):
        if jb < nb:
            weights(jb)
        if jb >= 1:
            outputs(jb - 1)


def _alibi_slopes():
    return jnp.asarray(np.array([2.0 ** (-8.0 * (h + 1) / B_HEADS) for h in range(B_HEADS)], np.float32))


def _moba_prompt(q, k, v, zs, batch, seq):
    shp = (batch, seq, D_B)
    spec = pl.BlockSpec((1, seq, B_HEAD_DIM), lambda b, h: (b, 0, h))
    nb = seq // MOBA_BLOCK
    n_score_cols = MOBA_BLOCK * (nb * (nb + 1) // 2)
    out = pl.pallas_call(
        functools.partial(_moba_prompt_kernel, seq=seq),
        grid=(batch, B_HEADS),
        in_specs=[pl.BlockSpec(memory_space=pltpu.SMEM), spec, spec, spec, spec],
        out_specs=spec,
        out_shape=jax.ShapeDtypeStruct(shp, BF16),
        scratch_shapes=[pltpu.VMEM((seq, B_HEAD_DIM + LANES), BF16)] * 3
                       + [pltpu.VMEM((MOBA_BLOCK, n_score_cols), F32),
                          pltpu.VMEM((MOBA_BLOCK, n_score_cols), BF16)],
        compiler_params=_params("arbitrary", "arbitrary"),
        name="moba_prompt",
    )(_alibi_slopes(), q.reshape(shp), k.reshape(shp), v.reshape(shp), zs.reshape(shp))
    return out.reshape(batch * seq, D_B)


CHUNK_PAGES = 8
K_RING = 4


def _moba_sample_kernel(pt_ref, slope_ref, trow_ref, q_ref, kn_ref, vn_ref, z_ref, ck_ref, cv_ref, o_ref,
                        kbuf, vbuf, ksem, vsem, isem, idx_v, idx_s,
                        qbd_scr, qbf_scr, cat_scr, s_scr, p_scr, ksum_scr, onew_scr, l_scr,
                        *, n_seq, n_new, n_pages):
    pg = CHUNK_PAGES
    nch = n_pages // pg
    total = n_seq * nch
    n_blocks = n_pages * PAGE_SIZE // MOBA_BLOCK
    blocks_per_chunk = pg * PAGE_SIZE // MOBA_BLOCK
    pages_per_block = MOBA_BLOCK // PAGE_SIZE
    rows = n_new * B_HEADS
    t = pl.program_id(0)
    head_of_lane = _div_pow2(lax.broadcasted_iota(jnp.int32, (B_HEADS, D_B), 1), B_HEAD_DIM)
    diag = head_of_lane == lax.broadcasted_iota(jnp.int32, (B_HEADS, D_B), 0)
    slope = slope_ref[:, 0:1]
    trow = trow_ref[:, 0:1]

    def k_copy(page, slot, r):
        return pltpu.make_async_copy(ck_ref.at[page], kbuf.at[slot, r], ksem.at[slot])

    def start_chunk(seq, ch, slot):
        for r in range(pg):
            k_copy(pt_ref[seq, ch * pg + r], slot, r).start()

    def v_copy(page, r, j, half):
        dst = vbuf.at[r, pl.ds((j * pages_per_block + half) * PAGE_SIZE, PAGE_SIZE), :]
        return pltpu.make_async_copy(cv_ref.at[page, :, r % B_HEADS, :], dst, vsem.at[0])

    @pl.when(t == 0)
    def _():
        for f0 in range(K_RING - 1):
            start_chunk(0, f0, f0)

    @pl.when(t < n_seq)
    def _():
        tiles = []
        for qi in range(n_new):
            qrow = q_ref[0, qi:qi + 1, :] * QK_SCALE
            tiles.append(jnp.where(diag, jnp.broadcast_to(qrow, (B_HEADS, D_B)), 0.0))
        qbd = jnp.concatenate(tiles, axis=0)
        qbf_scr[...] = qbd
        qbd_scr[...] = qbd.astype(BF16)

        def chunk_body(c, carry):
            f = t * nch + c
            ahead = f + (K_RING - 1)

            @pl.when(ahead < total)
            def _():
                start_chunk(lax.div(ahead, jnp.int32(nch)), lax.rem(ahead, jnp.int32(nch)),
                            lax.rem(ahead, jnp.int32(K_RING)))

            slot = lax.rem(f, jnp.int32(K_RING))
            for r in range(pg):
                k_copy(0, slot, r).wait()
            pair = None
            for r in range(pg):
                for hd in range(B_HEADS):
                    part = kbuf[slot, r, pl.ds(hd, PAGE_SIZE, stride=B_HEADS), :]
                    cat_scr[r * PAGE_SIZE:(r + 1) * PAGE_SIZE,
                            hd * B_HEAD_DIM:(hd + 1) * B_HEAD_DIM] = part.astype(BF16)
                ps = jnp.sum(kbuf[slot, r].reshape(PAGE_SIZE, B_HEADS, B_HEAD_DIM), axis=0)
                if r % pages_per_block == 0:
                    pair = ps
                else:
                    pair = pair + ps
                if r % pages_per_block == pages_per_block - 1:
                    ksum_scr[c * blocks_per_chunk + r // pages_per_block] = pair
            s_scr[c] = lax.dot_general(qbd_scr[...], cat_scr[...], NT_DIMS, preferred_element_type=F32)
            return carry

        lax.fori_loop(0, nch, chunk_body, 0)

    @pl.when(t >= 1)
    def _():
        for r in range(rows):
            for j in range(MOBA_TOPK):
                for half in range(pages_per_block):
                    v_copy(0, r, j, half).wait()
        head_rows = []
        for r in range(rows):
            parts = [p_scr[idx_s[r, j], r:r + 1, :] for j in range(MOBA_TOPK)]
            p_r = jnp.broadcast_to(jnp.concatenate(parts, axis=1), (SUBLANES, MOBA_TOPK * MOBA_BLOCK))
            o = jnp.dot(p_r.astype(BF16), vbuf[r].astype(BF16), preferred_element_type=F32)
            head_rows.append(o[0:1, :])
        out_rows = []
        for qi in range(n_new):
            sl = slice(qi * B_HEADS, (qi + 1) * B_HEADS)
            o_sel = jnp.concatenate(head_rows[sl], axis=1)
            o_new = jnp.sum(jnp.where(diag, onew_scr[sl, :], 0.0), axis=0, keepdims=True)
            l_row = jnp.concatenate([l_scr[r:r + 1, :] for r in range(sl.start, sl.stop)], axis=1)
            out_rows.append((o_sel + o_new) / l_row)
        o_ref[0] = jnp.concatenate(out_rows, axis=0) * z_ref[0]

    @pl.when(t < n_seq)
    def _():
        kbar = jnp.concatenate([ksum_scr[:, hd, :] for hd in range(B_HEADS)], axis=1)
        kbar = kbar * (1.0 / MOBA_BLOCK)
        g = lax.dot_general(qbf_scr[...], kbar, NT_DIMS, precision=HIGHEST,
                            preferred_element_type=F32)
        blk = lax.broadcasted_iota(jnp.int32, (rows, n_blocks), 1)
        sel, rank = _rank_select(g, blk >= 0, blk, 1, MOBA_TOPK)

        lane = lax.broadcasted_iota(jnp.int32, (rows, LANES), 1)
        ids = jnp.zeros((rows, LANES), F32)
        for j in range(MOBA_TOPK):
            col = jnp.sum(jnp.where(rank == float(j), blk.astype(F32), 0.0), axis=-1, keepdims=True)
            ids = jnp.where(lane == j, col, ids)
        idx_v[...] = ids.astype(jnp.int32)
        to_smem = pltpu.make_async_copy(idx_v, idx_s, isem.at[0])
        to_smem.start()
        to_smem.wait()
        for r in range(rows):
            for j in range(MOBA_TOPK):
                for half in range(pages_per_block):
                    v_copy(pt_ref[t, idx_s[r, j] * pages_per_block + half], r, j, half).start()

        key = lax.broadcasted_iota(jnp.int32, (rows, MOBA_BLOCK), 1).astype(F32)
        top = None
        for n in range(n_blocks):
            gi, a = divmod(n, blocks_per_chunk)
            dist = trow - (float(n * MOBA_BLOCK) + key)
            slab = s_scr[gi, :, a * MOBA_BLOCK:(a + 1) * MOBA_BLOCK] - slope * dist
            slab = jnp.where(sel[:, n:n + 1] > 0.5, slab, NEG)
            p_scr[n] = slab
            top = slab if top is None else jnp.maximum(top, slab)
        m = jnp.max(top, axis=-1, keepdims=True)
        knew = jnp.concatenate([kn_ref[0], jnp.zeros((LANES - n_new, D_B), F32)], axis=0).astype(BF16)
        s_new = lax.dot_general(qbd_scr[...], knew, NT_DIMS, preferred_element_type=F32)
        dist_new = trow - (float(n_pages * PAGE_SIZE) + lane.astype(F32))
        s_new = jnp.where(dist_new >= 0.0, s_new - slope * dist_new, NEG)
        m = jnp.maximum(m, jnp.max(s_new, axis=-1, keepdims=True))
        tot = jnp.zeros((rows, MOBA_BLOCK), F32)
        for n in range(n_blocks):
            p = jnp.exp(p_scr[n] - m)
            tot = tot + p
            p_scr[n] = p
        pn = jnp.exp(s_new - m)
        l = jnp.sum(tot, axis=-1, keepdims=True) + jnp.sum(pn, axis=-1, keepdims=True)
        l_scr[...] = jnp.broadcast_to(l, (rows, LANES))
        vnew = jnp.concatenate([vn_ref[0], jnp.zeros((LANES - n_new, D_B), F32)], axis=0).astype(BF16)
        onew_scr[...] = jnp.dot(pn.astype(BF16), vnew, preferred_element_type=F32)


def _moba_sample(q, kn, vn, zs, cache_k, cache_v, page_table, n_seq, n_new):
    n_pages = page_table.shape[1]
    n_phys = cache_k.shape[0]
    pg = CHUNK_PAGES
    nch = n_pages // pg
    assert n_pages % pg == 0 and K_RING - 1 <= nch
    rows = n_new * B_HEADS
    past = n_pages * PAGE_SIZE
    shp = (n_seq, n_new, D_B)
    slope_rows = np.tile(np.array([2.0 ** (-8.0 * (h + 1) / B_HEADS) for h in range(B_HEADS)], np.float32),
                         n_new)
    t_rows = np.repeat(past + np.arange(n_new, dtype=np.float32), B_HEADS)
    slope_rows = jnp.asarray(np.tile(slope_rows[:, None], (1, LANES)))
    t_rows = jnp.asarray(np.tile(t_rows[:, None], (1, LANES)))

    const = pl.BlockSpec((rows, LANES), lambda t, pt: (0, 0))
    cur = pl.BlockSpec((1, n_new, D_B), lambda t, pt: (jnp.minimum(t, n_seq - 1), 0, 0))
    prev = pl.BlockSpec((1, n_new, D_B), lambda t, pt: (jnp.maximum(t - 1, 0), 0, 0))
    hbm = pl.BlockSpec(memory_space=pl.ANY)

    page_rows = PAGE_SIZE * B_HEADS
    n_blocks = past // MOBA_BLOCK
    out = pl.pallas_call(
        functools.partial(_moba_sample_kernel, n_seq=n_seq, n_new=n_new, n_pages=n_pages),
        grid_spec=pltpu.PrefetchScalarGridSpec(
            num_scalar_prefetch=1,
            grid=(n_seq + 1,),
            in_specs=[const, const, cur, cur, cur, prev, hbm, hbm],
            out_specs=prev,
            scratch_shapes=[
                pltpu.VMEM((K_RING, pg, page_rows, B_HEAD_DIM), F32),
                pltpu.VMEM((rows, MOBA_TOPK * MOBA_BLOCK, B_HEAD_DIM), F32),
                pltpu.SemaphoreType.DMA((K_RING,)),
                pltpu.SemaphoreType.DMA((1,)),
                pltpu.SemaphoreType.DMA((1,)),
                pltpu.VMEM((rows, LANES), jnp.int32),
                pltpu.SMEM((rows, LANES), jnp.int32),
                pltpu.VMEM((rows, D_B), BF16),
                pltpu.VMEM((rows, D_B), F32),
                pltpu.VMEM((pg * PAGE_SIZE, D_B), BF16),
                pltpu.VMEM((nch, rows, pg * PAGE_SIZE), F32),
                pltpu.VMEM((n_blocks, rows, MOBA_BLOCK), F32),
                pltpu.VMEM((n_blocks, B_HEADS, B_HEAD_DIM), F32),
                pltpu.VMEM((rows, D_B), F32),
                pltpu.VMEM((rows, LANES), F32),
            ]),
        out_shape=jax.ShapeDtypeStruct(shp, F32),
        compiler_params=_params("arbitrary"),
        name="moba_sample",
    )(page_table, slope_rows, t_rows, q.reshape(shp), kn.reshape(shp), vn.reshape(shp), zs.reshape(shp),
      cache_k.reshape(n_phys, page_rows, B_HEAD_DIM), cache_v)
    return out.reshape(n_seq * n_new, D_B)


def _out_proj_kernel(ya_ref, yb_ref, w_ref, x_ref, gt_ref, g_ref, *rest, final):
    wbf_ref = rest[-1]

    @pl.when(pl.program_id(0) == 0)
    def _():
        wbf_ref[...] = w_ref[...].astype(BF16)

    tm, half = ya_ref.shape
    sub = min(tm, OUT_SUB_ROWS)
    mod_rows = _mod_rows
    for c in range(tm // sub):
        rs = slice(c * sub, (c + 1) * sub)
        acc = jnp.dot(ya_ref[rs, :].astype(BF16), wbf_ref[0:half, :], preferred_element_type=F32)
        acc = acc + jnp.dot(yb_ref[rs, :].astype(BF16), wbf_ref[half:, :], preferred_element_type=F32)
        x = x_ref[rs, :] + mod_rows(gt_ref, rs) * acc
        r = lax.rsqrt(jnp.mean(x * x, axis=-1, keepdims=True) + EPS)
        if final:
            o_ref = rest[0]
            o_ref[rs, :] = (x * r) * g_ref[...]
        else:
            sc_ref, sh_ref, x_out_ref, h_ref = rest[:4]
            x_out_ref[rs, :] = x
            h = (x * r) * g_ref[...] * (1.0 + mod_rows(sc_ref, rs)) + mod_rows(sh_ref, rs)
            h_ref[rs, :] = h.astype(h_ref.dtype)


def _out_proj(ya, yb, ya_col, yb_col, w, x, mod, g, next_mod, tm):
    rows = x.shape[0]
    half = D_MODEL // 2
    final = next_mod is None
    row_f32 = pl.BlockSpec((tm, D_MODEL), lambda i: (i, 0))
    in_specs = [
        pl.BlockSpec((tm, half), lambda i: (i, ya_col)),
        pl.BlockSpec((tm, half), lambda i: (i, yb_col)),
        pl.BlockSpec((D_MODEL, D_MODEL), lambda i: (0, 0), pipeline_mode=pl.Buffered(1)),
        row_f32,
        mod.spec(2, tm),
        pl.BlockSpec((1, D_MODEL), lambda i: (0, 0)),
    ]
    args = [ya, yb, w, x, mod.arr, g.reshape(1, D_MODEL)]
    if final:
        out_specs, out_shape = row_f32, jax.ShapeDtypeStruct((rows, D_MODEL), F32)
    else:
        in_specs += [next_mod.spec(1, tm), next_mod.spec(0, tm)]
        args += [next_mod.arr, next_mod.arr]
        out_specs = [row_f32, row_f32]
        out_shape = [jax.ShapeDtypeStruct((rows, D_MODEL), F32), jax.ShapeDtypeStruct((rows, D_MODEL), BF16)]
    return pl.pallas_call(
        functools.partial(_out_proj_kernel, final=final),
        grid=(rows // tm,),
        in_specs=in_specs,
        out_specs=out_specs,
        out_shape=out_shape,
        scratch_shapes=[pltpu.VMEM((D_MODEL, D_MODEL), BF16)],
        compiler_params=_params("arbitrary"),
        name="out_proj_final" if final else "out_proj",
    )(*args)


def _conv_proj_kernel(h_ref, wa_ref, wb_ref, wz_ref, glu_ref, sz_ref, wbf_ref):
    @pl.when(pl.program_id(1) == 0)
    def _():
        wbf_ref[0] = wa_ref[...].astype(BF16)
        wbf_ref[1] = wb_ref[...].astype(BF16)
        wbf_ref[2] = wz_ref[...].astype(BF16)

    h = h_ref[...]
    a = jnp.dot(h, wbf_ref[0], preferred_element_type=F32)
    b = jnp.dot(h, wbf_ref[1], preferred_element_type=F32)
    glu_ref[...] = a * _sigmoid(b)
    sz_ref[...] = _silu(jnp.dot(h, wbf_ref[2], preferred_element_type=F32))


def _conv_proj(h, w, tm):
    rows = h.shape[0]
    tn = 512
    nj = D_C // tn
    out = pl.BlockSpec((tm, tn), lambda j, i: (i, j))
    return pl.pallas_call(
        _conv_proj_kernel,
        grid=(nj, rows // tm),
        in_specs=[
            pl.BlockSpec((tm, D_MODEL), lambda j, i: (i, 0)),
            pl.BlockSpec((D_MODEL, tn), lambda j, i: (0, j)),
            pl.BlockSpec((D_MODEL, tn), lambda j, i: (0, nj + j)),
            pl.BlockSpec((D_MODEL, tn), lambda j, i: (0, 2 * nj + j)),
        ],
        out_specs=[out, out],
        out_shape=[jax.ShapeDtypeStruct((rows, D_C), F32)] * 2,
        scratch_shapes=[pltpu.VMEM((3, D_MODEL, tn), BF16)],
        compiler_params=_params("arbitrary", "arbitrary"),
        name="conv_proj",
    )(h, w, w, w)


HALO = 32
CONV_LANES = 128
LN_ROWS = 16


def _conv_ln_kernel(glu_ref, halo_ref, sz_ref, w_ref, cb_ref, g_ref, b_ref, o_ref, buf_scr, y_scr, wb_scr,
                    *, tiles_per_seq, tm):
    first = (pl.program_id(0) % tiles_per_seq) == 0

    @pl.when(pl.program_id(0) == 0)
    def _():
        for k in range(CONV_W):
            wb_scr[k] = jnp.broadcast_to(w_ref[k:k + 1, :], (SUBLANES, D_C))

    @pl.when(first)
    def _():
        buf_scr[0:HALO, :] = jnp.zeros((HALO, D_C), F32)

    @pl.when(jnp.logical_not(first))
    def _():
        buf_scr[0:HALO, :] = halo_ref[...]

    buf_scr[HALO:HALO + tm, :] = glu_ref[...]

    off = HALO - (CONV_W - 1)
    taps = [[(a, SUBLANES * a + r - off) for a in range((CONV_W + off) // SUBLANES + 1)
             if 0 <= SUBLANES * a + r - off < CONV_W] for r in range(SUBLANES)]
    rowid = lax.broadcasted_iota(jnp.int32, (SUBLANES, CONV_LANES), 0)

    for lc in range(D_C // CONV_LANES):
        lanes = slice(lc * CONV_LANES, (lc + 1) * CONV_LANES)

        def z_block(u0, r):
            acc = None
            for a, k in taps[r]:
                term = wb_scr[k, :, lanes] * buf_scr[pl.ds(u0 + SUBLANES * a, SUBLANES), lanes]
                acc = term if acc is None else acc + term
            return acc

        def body(blk, carry):
            t0 = pl.multiple_of(blk * SUBLANES, SUBLANES)
            terms = [z_block(t0, 0)]
            nxt = []
            for r in range(1, SUBLANES):
                zn = z_block(t0 + SUBLANES, r)
                mixed = jnp.where(rowid >= r, carry[r - 1], zn)
                terms.append(pltpu.roll(mixed, SUBLANES - r, 0))
                nxt.append(zn)
            while len(terms) > 1:
                terms = [terms[i] + terms[i + 1] for i in range(0, len(terms), 2)]
            y_scr[pl.ds(t0, SUBLANES), lanes] = terms[0]
            return tuple(nxt)

        init = tuple(z_block(0, r) for r in range(1, SUBLANES))
        lax.fori_loop(0, tm // SUBLANES, body, init, unroll=4)

    def norm_rows(blk, carry):
        r0 = pl.multiple_of(blk * LN_ROWS, LN_ROWS)
        rs = pl.ds(r0, LN_ROWS)
        y = _layernorm_rows(y_scr[rs, :] + cb_ref[...], g_ref[...], b_ref[...])
        o_ref[rs, :] = (_silu(y) * sz_ref[rs, :]).astype(o_ref.dtype)
        return carry

    lax.fori_loop(0, tm // LN_ROWS, norm_rows, 0, unroll=8)


def _conv_ln(glu, sz, conv_w, conv_b, ln_g, ln_b, seq, tm):
    rows = glu.shape[0]
    vec = pl.BlockSpec((1, D_C), lambda i: (0, 0))
    tile = pl.BlockSpec((tm, D_C), lambda i: (i, 0))
    return pl.pallas_call(
        functools.partial(_conv_ln_kernel, tiles_per_seq=seq // tm, tm=tm),
        grid=(rows // tm,),
        in_specs=[tile,
                  pl.BlockSpec((HALO, D_C), lambda i: (jnp.maximum(i * (tm // HALO) - 1, 0), 0)),
                  tile,
                  pl.BlockSpec((CONV_W, D_C), lambda i: (0, 0)),
                  vec, vec, vec],
        out_specs=tile,
        out_shape=jax.ShapeDtypeStruct((rows, D_C), BF16),
        scratch_shapes=[pltpu.VMEM((HALO + tm, D_C), F32), pltpu.VMEM((tm, D_C), F32),
                        pltpu.VMEM((CONV_W, SUBLANES, D_C), F32)],
        compiler_params=_params("arbitrary"),
        name="conv_ln",
    )(glu, glu, sz, conv_w, conv_b.reshape(1, D_C), ln_g.reshape(1, D_C), ln_b.reshape(1, D_C))


def _conv_ln_sample_kernel(st_ref, glu_ref, sz_ref, w_ref, cb_ref, g_ref, b_ref, o_ref, ns_ref, buf_scr,
                           *, n_new):
    hist = CONV_W - 1
    buf_scr[0:hist, :] = st_ref[0]
    buf_scr[hist:hist + n_new, :] = glu_ref[0]
    acc = jnp.zeros((n_new, D_C), F32) + cb_ref[...]
    for k in range(CONV_W):
        acc = acc + w_ref[k:k + 1, :] * buf_scr[k:k + n_new, :]
    y = _layernorm_rows(acc, g_ref[...], b_ref[...])
    o_ref[0] = _silu(y) * sz_ref[0]
    ns_ref[0] = buf_scr[n_new:n_new + hist, :]


def _conv_ln_sample(state, glu, sz, conv_w, conv_b, ln_g, ln_b, n_seq, n_new):
    hist = CONV_W - 1
    vec = pl.BlockSpec((1, D_C), lambda s: (0, 0))
    new = pl.BlockSpec((1, n_new, D_C), lambda s: (s, 0, 0))
    st = pl.BlockSpec((1, hist, D_C), lambda s: (s, 0, 0))
    y, ns = pl.pallas_call(
        functools.partial(_conv_ln_sample_kernel, n_new=n_new),
        grid=(n_seq,),
        in_specs=[st, new, new, pl.BlockSpec((CONV_W, D_C), lambda s: (0, 0)), vec, vec, vec],
        out_specs=[new, st],
        out_shape=[jax.ShapeDtypeStruct((n_seq, n_new, D_C), F32),
                   jax.ShapeDtypeStruct((n_seq, hist, D_C), F32)],
        scratch_shapes=[pltpu.VMEM((hist + n_new, D_C), F32)],
        compiler_params=_params("arbitrary"),
        name="conv_ln_sample",
    )(state, glu.reshape(n_seq, n_new, D_C), sz.reshape(n_seq, n_new, D_C), conv_w,
      conv_b.reshape(1, D_C), ln_g.reshape(1, D_C), ln_b.reshape(1, D_C))
    return y.reshape(n_seq * n_new, D_C), ns


def _trunk(x, mods, tm, seq, weights, attend, conv):
    (norm_g, ab_w_in, a_ln_g, a_ln_b, a_ws, a_bs, ab_w_out, c_w_in, c_w_out, final_norm_g) = weights
    tm_row = min(tm, ROW_TILE)
    u, h = _proj_norm(x, norm_g[0], mods[0], ab_w_in[0], tm)
    seg = lambda s, ep: _proj_seg(h, ab_w_in[0], s, ep, a_ln_g[0], a_ln_b[0], tm)
    vn = seg(1, "gelu_ln")
    q, zb = seg(3, "none"), seg(6, "silu")
    (k, k_heads), (v, v_heads) = seg(4, "heads"), seg(5, "heads")
    ya = _a_mix(h, ab_w_in[0], 2, u, vn, a_ws[0], a_bs[0], seq, tm)
    yb = attend(q, k, v, zb)
    x1, h1 = _out_proj(ya, yb, 0, 0, ab_w_out[0], x, mods[0], norm_g[1], mods[1], tm_row)
    glu, sz = _conv_proj(h1, c_w_in[0], tm)
    y2, conv_state = conv(glu, sz)
    y = _out_proj(y2, y2, 0, 1, c_w_out[0], x1, mods[1], final_norm_g, None, tm_row)
    return y, k_heads, v_heads, vn, conv_state


def kernel(x_prompt, x_sample, cache_b_k, cache_b_v, state_c_conv, page_table, c_prompt, c_sample,
           ada_w, ada_b, norm_g, ab_w_in, a_ln_g, a_ln_b, a_ws, a_bs, ab_w_out,
           c_w_in, c_conv_w, c_conv_b, c_ln_g, c_ln_b, c_w_out, final_norm_g):
    batch, seq, _ = x_prompt.shape
    n_seq, n_new, _ = x_sample.shape
    weights = (norm_g, ab_w_in, a_ln_g, a_ln_b, a_ws, a_bs, ab_w_out, c_w_in, c_w_out, final_norm_g)

    mod = _adaln(jnp.concatenate([c_prompt, c_sample], axis=0), ada_w, ada_b)
    mod = mod.reshape(DEPTH, batch + n_seq, 3, D_MODEL)
    tm_p = PROJ_TILE
    mods_p, mods_s = [], []
    for l in range(DEPTH):
        mods_p.append(_Mod(mod[l, :batch].reshape(batch * 3, 1, D_MODEL), seq))
        per_row = jnp.repeat(mod[l, batch:], n_new, axis=0).transpose(1, 0, 2)
        mods_s.append(_Mod(per_row))

    yp, kp, vp, _, glu_p = _trunk(
        x_prompt.reshape(batch * seq, D_MODEL), mods_p, tm_p, seq, weights,
        attend=lambda q, k, v, zb: _moba_prompt(q, k, v, zb, batch, seq),
        conv=lambda glu, sz: (_conv_ln(glu, sz, c_conv_w[0], c_conv_b[0], c_ln_g[0], c_ln_b[0], seq, ROW_TILE), glu))
    ys, ks, vs, vas, conv_s = _trunk(
        x_sample.reshape(n_seq * n_new, D_MODEL), mods_s, n_seq * n_new, n_new, weights,
        attend=lambda q, k, v, zb: _moba_sample(q, k, v, zb, cache_b_k[0], cache_b_v[0], page_table,
                                               n_seq, n_new),
        conv=lambda glu, sz: _conv_ln_sample(state_c_conv[0], glu, sz, c_conv_w[0], c_conv_b[0],
                                             c_ln_g[0], c_ln_b[0], n_seq, n_new))

    hd = (B_HEADS, B_HEAD_DIM)
    conv_p = glu_p.reshape(batch, seq, D_C)[:, seq - (CONV_W - 1):, :]
    return (yp.reshape(batch, seq, D_MODEL), ys.reshape(n_seq, n_new, D_MODEL),
            kp.reshape(1, batch, seq, *hd), vp.reshape(1, batch, seq, *hd),
            ks.reshape(1, n_seq, n_new, *hd), vs.reshape(1, n_seq, n_new, *hd),
            vas.reshape(1, n_seq, n_new, D_A), conv_p[None], conv_s[None])
```

```python
import functools

import numpy as np
import jax
import jax.numpy as jnp
from jax import lax
from jax.experimental import pallas as pl
from jax.experimental.pallas import tpu as pltpu

F32 = jnp.float32
BF16 = jnp.bfloat16
HIGHEST = lax.Precision.HIGHEST

D_MODEL = 2048
DEPTH = 2
PAST_LEN = 8192
PAGE_SIZE = 128
D_A = 1024
A_GROUPS = 8
A_CHUNK = 128
B_HEADS = 8
B_HEAD_DIM = 128
D_B = B_HEADS * B_HEAD_DIM
MOBA_BLOCK = 256
MOBA_TOPK = 3
D_C = D_MODEL
CONV_W = 31
EPS = 1e-6
NEG = -1e30
QK_SCALE = B_HEAD_DIM ** -0.5

VMEM_LIMIT_BYTES = 56 * 1024 * 1024
SUBLANES = 8
LANES = 128
PROJ_TILE = 1024
ROW_TILE = 512
OUT_SUB_ROWS = 256

NT_DIMS = (((1,), (1,)), ((), ()))


def _params(*sem):
    return pltpu.CompilerParams(dimension_semantics=sem, vmem_limit_bytes=VMEM_LIMIT_BYTES)


def _div_pow2(x, d):
    shift = int(d).bit_length() - 1
    assert 1 << shift == d
    return lax.shift_right_logical(x, jnp.int32(shift))


def _sigmoid(x):
    return jax.nn.sigmoid(x)


def _silu(x):
    return x * _sigmoid(x)


def _gelu_tanh(x):
    c = np.float32(np.sqrt(2.0 / np.pi))
    return x * (0.5 * (1.0 + jnp.tanh(c * (x + 0.044715 * (x * x * x)))))


def _layernorm_rows(x, g, b):
    mu = jnp.mean(x, axis=-1, keepdims=True)
    xc = x - mu
    var = jnp.mean(xc * xc, axis=-1, keepdims=True)
    return (xc * lax.rsqrt(var + EPS)) * g + b


def _adaln_kernel(c_ref, w_ref, b_ref, o_ref):
    s = _silu(c_ref[...]).astype(BF16)
    o_ref[0] = jnp.dot(s, w_ref[0].astype(BF16), preferred_element_type=F32) + b_ref[0]


def _adaln(c_all, ada_w, ada_b):
    n = c_all.shape[0]
    tn = 1024
    return pl.pallas_call(
        _adaln_kernel,
        grid=(DEPTH, 3 * D_MODEL // tn),
        in_specs=[
            pl.BlockSpec((n, D_MODEL), lambda l, j: (0, 0)),
            pl.BlockSpec((1, D_MODEL, tn), lambda l, j: (l, 0, j)),
            pl.BlockSpec((1, 1, tn), lambda l, j: (l, 0, j)),
        ],
        out_specs=pl.BlockSpec((1, n, tn), lambda l, j: (l, 0, j)),
        out_shape=jax.ShapeDtypeStruct((DEPTH, n, 3 * D_MODEL), F32),
        compiler_params=_params("arbitrary", "arbitrary"),
        name="adaln",
    )(c_all, ada_w, ada_b.reshape(DEPTH, 1, 3 * D_MODEL))


class _Mod:
    def __init__(self, arr, seq_len=None):
        self.arr = arr
        self.rows = arr.shape[1]
        self.seq_len = seq_len

    def spec(self, which, tm):
        if self.seq_len is None:
            return pl.BlockSpec((1, self.rows, D_MODEL), lambda i: (which, 0, 0))
        assert self.seq_len % tm == 0
        tiles = self.seq_len // tm
        return pl.BlockSpec((1, 1, D_MODEL), lambda i: ((i // tiles) * 3 + which, 0, 0))


def _mod_rows(ref, rs):
    return ref[0] if ref.shape[1] == 1 else ref[0, rs, :]


def _proj_norm_kernel(x_ref, ng_ref, sc_ref, sh_ref, w_ref, u_ref, h_ref, wbf_ref):
    @pl.when(pl.program_id(0) == 0)
    def _():
        wbf_ref[...] = w_ref[...].astype(BF16)

    tm = x_ref.shape[0]
    sub = min(tm, OUT_SUB_ROWS)
    for c in range(tm // sub):
        rs = slice(c * sub, (c + 1) * sub)
        x = x_ref[rs, :]
        r = lax.rsqrt(jnp.mean(x * x, axis=-1, keepdims=True) + EPS)
        h = ((x * r) * ng_ref[...] * (1.0 + _mod_rows(sc_ref, rs)) + _mod_rows(sh_ref, rs)).astype(BF16)
        h_ref[rs, :] = h
        u_ref[rs, :] = _gelu_tanh(jnp.dot(h, wbf_ref[...], preferred_element_type=F32))


def _proj_norm(x, g, mod, w, tm):
    rows = x.shape[0]
    tn = 1024
    return pl.pallas_call(
        _proj_norm_kernel,
        grid=(rows // tm,),
        in_specs=[
            pl.BlockSpec((tm, D_MODEL), lambda i: (i, 0)),
            pl.BlockSpec((1, D_MODEL), lambda i: (0, 0)),
            mod.spec(1, tm),
            mod.spec(0, tm),
            pl.BlockSpec((D_MODEL, tn), lambda i: (0, 0), pipeline_mode=pl.Buffered(1)),
        ],
        out_specs=[pl.BlockSpec((tm, tn), lambda i: (i, 0)), pl.BlockSpec((tm, D_MODEL), lambda i: (i, 0))],
        out_shape=[jax.ShapeDtypeStruct((rows, tn), F32), jax.ShapeDtypeStruct((rows, D_MODEL), BF16)],
        scratch_shapes=[pltpu.VMEM((D_MODEL, tn), BF16)],
        compiler_params=_params("arbitrary"),
        name="proj_norm_gelu",
    )(x, g.reshape(1, D_MODEL), mod.arr, mod.arr, w)


def _proj_seg_kernel(h_ref, w_ref, g_ref, b_ref, *rest, epilogue):
    o_ref, wbf_ref = rest[0], rest[-1]

    @pl.when(pl.program_id(0) == 0)
    def _():
        wbf_ref[...] = w_ref[...].astype(BF16)

    acc = jnp.dot(h_ref[...], wbf_ref[...], preferred_element_type=F32)
    if epilogue == "gelu_ln":
        acc = _layernorm_rows(_gelu_tanh(acc), g_ref[...], b_ref[...])
    elif epilogue == "silu":
        acc = _silu(acc)
    o_ref[...] = acc
    if epilogue == "heads":
        oh_ref = rest[1]
        tm = acc.shape[0]
        for hd in range(B_HEADS):
            oh_ref[pl.ds(hd, tm, stride=B_HEADS), :] = acc[:, hd * B_HEAD_DIM:(hd + 1) * B_HEAD_DIM]


def _proj_seg(h, w, seg, epilogue, ln_g, ln_b, tm):
    rows = h.shape[0]
    tn = 1024
    out_specs = [pl.BlockSpec((tm, tn), lambda i: (i, 0))]
    out_shape = [jax.ShapeDtypeStruct((rows, tn), F32)]
    if epilogue == "heads":
        out_specs.append(pl.BlockSpec((tm * B_HEADS, B_HEAD_DIM), lambda i: (i, 0)))
        out_shape.append(jax.ShapeDtypeStruct((rows * B_HEADS, B_HEAD_DIM), F32))
    out = pl.pallas_call(
        functools.partial(_proj_seg_kernel, epilogue=epilogue),
        grid=(rows // tm,),
        in_specs=[
            pl.BlockSpec((tm, D_MODEL), lambda i: (i, 0)),
            pl.BlockSpec((D_MODEL, tn), lambda i: (0, seg), pipeline_mode=pl.Buffered(1)),
            pl.BlockSpec((1, tn), lambda i: (0, 0)),
            pl.BlockSpec((1, tn), lambda i: (0, 0)),
        ],
        out_specs=out_specs,
        out_shape=out_shape,
        scratch_shapes=[pltpu.VMEM((D_MODEL, tn), BF16)],
        compiler_params=_params("arbitrary"),
        name="proj_" + epilogue,
    )(h, w, ln_g.reshape(1, tn), ln_b.reshape(1, tn))
    return out if epilogue == "heads" else out[0]


def _a_mix_kernel(h_ref, wz_ref, u_ref, v_ref, w_ref, bst_ref, o_ref, wbf_ref, z_ref, *, seq_len):
    @pl.when(pl.program_id(0) == 0)
    def _():
        wbf_ref[...] = wz_ref[...].astype(BF16)

    z_ref[...] = _silu(jnp.dot(h_ref[...], wbf_ref[...], preferred_element_type=F32))
    tm = h_ref.shape[0]
    row = lax.broadcasted_iota(jnp.int32, (A_CHUNK, A_CHUNK), 0)
    col = lax.broadcasted_iota(jnp.int32, (A_CHUNK, A_CHUNK), 1)
    keep = jnp.where(col <= row, 1.0, 0.0)
    if seq_len < A_CHUNK:
        keep = keep * jnp.where(_div_pow2(row, seq_len) == _div_pow2(col, seq_len), 1.0, 0.0)
    gw = D_A // A_GROUPS
    for g in range(A_GROUPS):
        wg = (w_ref[g] * keep).astype(BF16)
        bias = bst_ref[:, g:g + 1]
        for c in range(tm // A_CHUNK):
            rs = slice(c * A_CHUNK, (c + 1) * A_CHUNK)
            cs = slice(g * gw, (g + 1) * gw)
            mix = jnp.dot(wg, v_ref[rs, cs].astype(BF16), preferred_element_type=F32) + bias
            o_ref[rs, cs] = ((u_ref[rs, cs] * mix) * z_ref[rs, cs]).astype(o_ref.dtype)


def _a_mix(h, w, seg, u, vn, ws, bs, seq_len, tm):
    rows = u.shape[0]
    n = min(seq_len, A_CHUNK)
    reps = A_CHUNK // n
    wmat = jnp.tile(ws[:, :n, :n], (1, reps, reps))
    bst = jnp.tile(bs[:, :n], (1, reps)).T
    spec = pl.BlockSpec((tm, D_A), lambda i: (i, 0))
    return pl.pallas_call(
        functools.partial(_a_mix_kernel, seq_len=n),
        grid=(rows // tm,),
        in_specs=[pl.BlockSpec((tm, D_MODEL), lambda i: (i, 0)),
                  pl.BlockSpec((D_MODEL, D_A), lambda i: (0, seg), pipeline_mode=pl.Buffered(1)),
                  spec, spec,
                  pl.BlockSpec((A_GROUPS, A_CHUNK, A_CHUNK), lambda i: (0, 0, 0)),
                  pl.BlockSpec((A_CHUNK, A_GROUPS), lambda i: (0, 0))],
        out_specs=spec,
        out_shape=jax.ShapeDtypeStruct((rows, D_A), BF16),
        scratch_shapes=[pltpu.VMEM((D_MODEL, D_A), BF16), pltpu.VMEM((tm, D_A), F32)],
        compiler_params=_params("arbitrary"),
        name="proj_silu_a_mix",
    )(h, w, u, vn, wmat, bst)


def _rank_select(g, valid, index, axis, count):
    gm = jnp.where(valid, g, NEG)
    n = g.shape[axis]
    rank = jnp.zeros(g.shape, F32)
    for m in range(n):
        ref = gm[m:m + 1, :] if axis == 0 else gm[:, m:m + 1]
        beats = jnp.where(ref > gm, 1.0, jnp.where(ref == gm, jnp.where(m < index, 1.0, 0.0), 0.0))
        rank = rank + beats
    return jnp.where(valid, jnp.where(rank < float(count), 1.0, 0.0), 0.0), rank


POS_SPLIT = 16


def _moba_prompt_kernel(slope_ref, q_ref, k_ref, v_ref, z_ref, o_ref, qa_scr, ka_scr, va_scr, s_scr, p_scr,
                        *, seq):
    nb = seq // MOBA_BLOCK
    assert nb + 2 <= LANES and seq // POS_SPLIT <= 256
    slope = slope_ref[pl.program_id(1)]

    @pl.when((pl.program_id(0) == 0) & (pl.program_id(1) == 0))
    def _():
        pos = lax.broadcasted_iota(jnp.int32, (seq, LANES), 0)
        lane = lax.broadcasted_iota(jnp.int32, (seq, LANES), 1)
        onehot = jnp.where(lane == _div_pow2(pos, MOBA_BLOCK), 1.0, 0.0)
        hi = _div_pow2(pos, POS_SPLIT).astype(F32)
        lo = (pos - _div_pow2(pos, POS_SPLIT) * POS_SPLIT).astype(F32)
        aug = jnp.where(lane < nb, onehot, jnp.where(lane == nb, hi, jnp.where(lane == nb + 1, lo, 0.0)))
        ka_scr[:, B_HEAD_DIM:] = aug.astype(BF16)
        va_scr[:, B_HEAD_DIM:] = jnp.ones((seq, LANES), BF16)

    k = k_ref[0]
    ka_scr[:, 0:B_HEAD_DIM] = k.astype(BF16)
    va_scr[:, 0:B_HEAD_DIM] = v_ref[0].astype(BF16)

    kbar = jnp.mean(k.reshape(nb, MOBA_BLOCK, B_HEAD_DIM), axis=1)
    qf = q_ref[0] * QK_SCALE
    gt = lax.dot_general(kbar, qf, NT_DIMS, precision=HIGHEST, preferred_element_type=F32)
    blk = lax.broadcasted_iota(jnp.int32, (nb, seq), 0)
    own = _div_pow2(lax.broadcasted_iota(jnp.int32, (nb, seq), 1), MOBA_BLOCK)
    sel, _ = _rank_select(gt, blk < own, blk, 0, MOBA_TOPK)
    pen = jnp.where(sel > 0.5, 0.0, jnp.where(blk == own, 0.0, NEG))
    pen = jnp.concatenate([pen, jnp.zeros((LANES - nb, seq), F32)], axis=0)
    row = lax.broadcasted_iota(jnp.int32, (LANES, seq), 0)
    aug_t = jnp.where(row == nb, POS_SPLIT * slope, jnp.where(row == nb + 1, slope, pen))
    qa_scr[:, B_HEAD_DIM:] = aug_t.T.astype(BF16)
    qa_scr[:, 0:B_HEAD_DIM] = qf.astype(BF16)

    causal = (lax.broadcasted_iota(jnp.int32, (MOBA_BLOCK, MOBA_BLOCK), 0)
              >= lax.broadcasted_iota(jnp.int32, (MOBA_BLOCK, MOBA_BLOCK), 1))
    offs = [MOBA_BLOCK * (jb * (jb + 1) // 2) for jb in range(nb + 1)]

    def scores(jb):
        rows = slice(jb * MOBA_BLOCK, (jb + 1) * MOBA_BLOCK)
        nk = (jb + 1) * MOBA_BLOCK
        s_scr[:, offs[jb]:offs[jb + 1]] = lax.dot_general(
            qa_scr[rows, :], ka_scr[0:nk, :], NT_DIMS, preferred_element_type=F32)

    def weights(jb):
        n_past = jb * MOBA_BLOCK
        own_cols = slice(offs[jb] + n_past, offs[jb + 1])
        own_s = jnp.where(causal, s_scr[:, own_cols], NEG)
        m = jnp.max(own_s, axis=-1, keepdims=True)
        if jb > 0:
            past_cols = slice(offs[jb], offs[jb] + n_past)
            m = jnp.maximum(m, jnp.max(s_scr[:, past_cols], axis=-1, keepdims=True))
            p_scr[:, past_cols] = jnp.exp(s_scr[:, past_cols] - m).astype(BF16)
        p_scr[:, own_cols] = jnp.exp(own_s - m).astype(BF16)

    def outputs(jb):
        rows = slice(jb * MOBA_BLOCK, (jb + 1) * MOBA_BLOCK)
        nk = (jb + 1) * MOBA_BLOCK
        o = jnp.dot(p_scr[:, offs[jb]:offs[jb + 1]], va_scr[0:nk, :], preferred_element_type=F32)
        y = o[:, 0:B_HEAD_DIM] / o[:, B_HEAD_DIM:B_HEAD_DIM + 1]
        o_ref[0, rows, :] = (y * z_ref[0, rows, :]).astype(o_ref.dtype)

    for jb in range(nb):
        scores(jb)
    for jb in range(nb + 1):
        if jb < nb:
            weights(jb)
        if jb >= 1:
            outputs(jb - 1)


def _alibi_slopes():
    return jnp.asarray(np.array([2.0 ** (-8.0 * (h + 1) / B_HEADS) for h in range(B_HEADS)], np.float32))


def _moba_prompt(q, k, v, zs, batch, seq):
    shp = (batch, seq, D_B)
    spec = pl.BlockSpec((1, seq, B_HEAD_DIM), lambda b, h: (b, 0, h))
    nb = seq // MOBA_BLOCK
    n_score_cols = MOBA_BLOCK * (nb * (nb + 1) // 2)
    out = pl.pallas_call(
        functools.partial(_moba_prompt_kernel, seq=seq),
        grid=(batch, B_HEADS),
        in_specs=[pl.BlockSpec(memory_space=pltpu.SMEM), spec, spec, spec, spec],
        out_specs=spec,
        out_shape=jax.ShapeDtypeStruct(shp, BF16),
        scratch_shapes=[pltpu.VMEM((seq, B_HEAD_DIM + LANES), BF16)] * 3
                       + [pltpu.VMEM((MOBA_BLOCK, n_score_cols), F32),
                          pltpu.VMEM((MOBA_BLOCK, n_score_cols), BF16)],
        compiler_params=_params("arbitrary", "arbitrary"),
        name="moba_prompt",
    )(_alibi_slopes(), q.reshape(shp), k.reshape(shp), v.reshape(shp), zs.reshape(shp))
    return out.reshape(batch * seq, D_B)


CHUNK_PAGES = 8
K_RING = 5


def _moba_sample_kernel(pt_ref, slope_ref, trow_ref, q_ref, kn_ref, vn_ref, z_ref, ck_ref, cv_ref, o_ref,
                        kbuf, vbuf, ksem, vsem, isem, idx_v, idx_s,
                        qbd_scr, qbf_scr, cat_scr, s_scr, p_scr, ksum_scr, onew_scr, l_scr,
                        *, n_seq, n_new, n_pages):
    pg = CHUNK_PAGES
    nch = n_pages // pg
    total = n_seq * nch
    n_blocks = n_pages * PAGE_SIZE // MOBA_BLOCK
    blocks_per_chunk = pg * PAGE_SIZE // MOBA_BLOCK
    pages_per_block = MOBA_BLOCK // PAGE_SIZE
    rows = n_new * B_HEADS
    t = pl.program_id(0)
    head_of_lane = _div_pow2(lax.broadcasted_iota(jnp.int32, (B_HEADS, D_B), 1), B_HEAD_DIM)
    diag = head_of_lane == lax.broadcasted_iota(jnp.int32, (B_HEADS, D_B), 0)
    slope = slope_ref[:, 0:1]
    trow = trow_ref[:, 0:1]

    def k_copy(page, slot, r):
        return pltpu.make_async_copy(ck_ref.at[page], kbuf.at[slot, r], ksem.at[slot])

    def start_chunk(seq, ch, slot):
        for r in range(pg):
            k_copy(pt_ref[seq, ch * pg + r], slot, r).start()

    def v_copy(page, r, j, half):
        dst = vbuf.at[r, pl.ds((j * pages_per_block + half) * PAGE_SIZE, PAGE_SIZE), :]
        return pltpu.make_async_copy(cv_ref.at[page, :, r % B_HEADS, :], dst, vsem.at[0])

    @pl.when(t == 0)
    def _():
        for f0 in range(K_RING - 1):
            start_chunk(0, f0, f0)

    @pl.when(t < n_seq)
    def _():
        tiles = []
        for qi in range(n_new):
            qrow = q_ref[0, qi:qi + 1, :] * QK_SCALE
            tiles.append(jnp.where(diag, jnp.broadcast_to(qrow, (B_HEADS, D_B)), 0.0))
        qbd = jnp.concatenate(tiles, axis=0)
        qbf_scr[...] = qbd
        qbd_scr[...] = qbd.astype(BF16)

        def chunk_body(c, carry):
            f = t * nch + c
            ahead = f + (K_RING - 1)

            @pl.when(ahead < total)
            def _():
                start_chunk(lax.div(ahead, jnp.int32(nch)), lax.rem(ahead, jnp.int32(nch)),
                            lax.rem(ahead, jnp.int32(K_RING)))

            slot = lax.rem(f, jnp.int32(K_RING))
            for r in range(pg):
                k_copy(0, slot, r).wait()
            pair = None
            for r in range(pg):
                for hd in range(B_HEADS):
                    part = kbuf[slot, r, pl.ds(hd, PAGE_SIZE, stride=B_HEADS), :]
                    cat_scr[r * PAGE_SIZE:(r + 1) * PAGE_SIZE,
                            hd * B_HEAD_DIM:(hd + 1) * B_HEAD_DIM] = part.astype(BF16)
                ps = jnp.sum(kbuf[slot, r].reshape(PAGE_SIZE, B_HEADS, B_HEAD_DIM), axis=0)
                if r % pages_per_block == 0:
                    pair = ps
                else:
                    pair = pair + ps
                if r % pages_per_block == pages_per_block - 1:
                    ksum_scr[c * blocks_per_chunk + r // pages_per_block] = pair
            s_scr[c] = lax.dot_general(qbd_scr[...], cat_scr[...], NT_DIMS, preferred_element_type=F32)
            return carry

        lax.fori_loop(0, nch, chunk_body, 0)

    @pl.when(t >= 1)
    def _():
        for r in range(rows):
            for j in range(MOBA_TOPK):
                for half in range(pages_per_block):
                    v_copy(0, r, j, half).wait()
        head_rows = []
        for r in range(rows):
            parts = [p_scr[idx_s[r, j], r:r + 1, :] for j in range(MOBA_TOPK)]
            p_r = jnp.broadcast_to(jnp.concatenate(parts, axis=1), (SUBLANES, MOBA_TOPK * MOBA_BLOCK))
            o = jnp.dot(p_r.astype(BF16), vbuf[r].astype(BF16), preferred_element_type=F32)
            head_rows.append(o[0:1, :])
        out_rows = []
        for qi in range(n_new):
            sl = slice(qi * B_HEADS, (qi + 1) * B_HEADS)
            o_sel = jnp.concatenate(head_rows[sl], axis=1)
            o_new = jnp.sum(jnp.where(diag, onew_scr[sl, :], 0.0), axis=0, keepdims=True)
            l_row = jnp.concatenate([l_scr[r:r + 1, :] for r in range(sl.start, sl.stop)], axis=1)
            out_rows.append((o_sel + o_new) / l_row)
        o_ref[0] = jnp.concatenate(out_rows, axis=0) * z_ref[0]

    @pl.when(t < n_seq)
    def _():
        kbar = jnp.concatenate([ksum_scr[:, hd, :] for hd in range(B_HEADS)], axis=1)
        kbar = kbar * (1.0 / MOBA_BLOCK)
        g = lax.dot_general(qbf_scr[...], kbar, NT_DIMS, precision=HIGHEST,
                            preferred_element_type=F32)
        blk = lax.broadcasted_iota(jnp.int32, (rows, n_blocks), 1)
        sel, rank = _rank_select(g, blk >= 0, blk, 1, MOBA_TOPK)

        lane = lax.broadcasted_iota(jnp.int32, (rows, LANES), 1)
        ids = jnp.zeros((rows, LANES), F32)
        for j in range(MOBA_TOPK):
            col = jnp.sum(jnp.where(rank == float(j), blk.astype(F32), 0.0), axis=-1, keepdims=True)
            ids = jnp.where(lane == j, col, ids)
        idx_v[...] = ids.astype(jnp.int32)
        to_smem = pltpu.make_async_copy(idx_v, idx_s, isem.at[0])
        to_smem.start()
        to_smem.wait()
        for r in range(rows):
            for j in range(MOBA_TOPK):
                for half in range(pages_per_block):
                    v_copy(pt_ref[t, idx_s[r, j] * pages_per_block + half], r, j, half).start()

        key = lax.broadcasted_iota(jnp.int32, (rows, MOBA_BLOCK), 1).astype(F32)
        top = None
        for n in range(n_blocks):
            gi, a = divmod(n, blocks_per_chunk)
            dist = trow - (float(n * MOBA_BLOCK) + key)
            slab = s_scr[gi, :, a * MOBA_BLOCK:(a + 1) * MOBA_BLOCK] - slope * dist
            slab = jnp.where(sel[:, n:n + 1] > 0.5, slab, NEG)
            p_scr[n] = slab
            top = slab if top is None else jnp.maximum(top, slab)
        m = jnp.max(top, axis=-1, keepdims=True)
        knew = jnp.concatenate([kn_ref[0], jnp.zeros((LANES - n_new, D_B), F32)], axis=0).astype(BF16)
        s_new = lax.dot_general(qbd_scr[...], knew, NT_DIMS, preferred_element_type=F32)
        dist_new = trow - (float(n_pages * PAGE_SIZE) + lane.astype(F32))
        s_new = jnp.where(dist_new >= 0.0, s_new - slope * dist_new, NEG)
        m = jnp.maximum(m, jnp.max(s_new, axis=-1, keepdims=True))
        tot = jnp.zeros((rows, MOBA_BLOCK), F32)
        for n in range(n_blocks):
            p = jnp.exp(p_scr[n] - m)
            tot = tot + p
            p_scr[n] = p
        pn = jnp.exp(s_new - m)
        l = jnp.sum(tot, axis=-1, keepdims=True) + jnp.sum(pn, axis=-1, keepdims=True)
        l_scr[...] = jnp.broadcast_to(l, (rows, LANES))
        vnew = jnp.concatenate([vn_ref[0], jnp.zeros((LANES - n_new, D_B), F32)], axis=0).astype(BF16)
        onew_scr[...] = jnp.dot(pn.astype(BF16), vnew, preferred_element_type=F32)


def _moba_sample(q, kn, vn, zs, cache_k, cache_v, page_table, n_seq, n_new):
    n_pages = page_table.shape[1]
    n_phys = cache_k.shape[0]
    pg = CHUNK_PAGES
    nch = n_pages // pg
    assert n_pages % pg == 0 and K_RING - 1 <= nch
    rows = n_new * B_HEADS
    past = n_pages * PAGE_SIZE
    shp = (n_seq, n_new, D_B)
    slope_rows = np.tile(np.array([2.0 ** (-8.0 * (h + 1) / B_HEADS) for h in range(B_HEADS)], np.float32),
                         n_new)
    t_rows = np.repeat(past + np.arange(n_new, dtype=np.float32), B_HEADS)
    slope_rows = jnp.asarray(np.tile(slope_rows[:, None], (1, LANES)))
    t_rows = jnp.asarray(np.tile(t_rows[:, None], (1, LANES)))

    const = pl.BlockSpec((rows, LANES), lambda t, pt: (0, 0))
    cur = pl.BlockSpec((1, n_new, D_B), lambda t, pt: (jnp.minimum(t, n_seq - 1), 0, 0))
    prev = pl.BlockSpec((1, n_new, D_B), lambda t, pt: (jnp.maximum(t - 1, 0), 0, 0))
    hbm = pl.BlockSpec(memory_space=pl.ANY)

    page_rows = PAGE_SIZE * B_HEADS
    n_blocks = past // MOBA_BLOCK
    out = pl.pallas_call(
        functools.partial(_moba_sample_kernel, n_seq=n_seq, n_new=n_new, n_pages=n_pages),
        grid_spec=pltpu.PrefetchScalarGridSpec(
            num_scalar_prefetch=1,
            grid=(n_seq + 1,),
            in_specs=[const, const, cur, cur, cur, prev, hbm, hbm],
            out_specs=prev,
            scratch_shapes=[
                pltpu.VMEM((K_RING, pg, page_rows, B_HEAD_DIM), F32),
                pltpu.VMEM((rows, MOBA_TOPK * MOBA_BLOCK, B_HEAD_DIM), F32),
                pltpu.SemaphoreType.DMA((K_RING,)),
                pltpu.SemaphoreType.DMA((1,)),
                pltpu.SemaphoreType.DMA((1,)),
                pltpu.VMEM((rows, LANES), jnp.int32),
                pltpu.SMEM((rows, LANES), jnp.int32),
                pltpu.VMEM((rows, D_B), BF16),
                pltpu.VMEM((rows, D_B), F32),
                pltpu.VMEM((pg * PAGE_SIZE, D_B), BF16),
                pltpu.VMEM((nch, rows, pg * PAGE_SIZE), F32),
                pltpu.VMEM((n_blocks, rows, MOBA_BLOCK), F32),
                pltpu.VMEM((n_blocks, B_HEADS, B_HEAD_DIM), F32),
                pltpu.VMEM((rows, D_B), F32),
                pltpu.VMEM((rows, LANES), F32),
            ]),
        out_shape=jax.ShapeDtypeStruct(shp, F32),
        compiler_params=_params("arbitrary"),
        name="moba_sample",
    )(page_table, slope_rows, t_rows, q.reshape(shp), kn.reshape(shp), vn.reshape(shp), zs.reshape(shp),
      cache_k.reshape(n_phys, page_rows, B_HEAD_DIM), cache_v)
    return out.reshape(n_seq * n_new, D_B)


def _out_proj_kernel(ya_ref, yb_ref, w_ref, x_ref, gt_ref, g_ref, *rest, final):
    wbf_ref = rest[-1]

    @pl.when(pl.program_id(0) == 0)
    def _():
        wbf_ref[...] = w_ref[...].astype(BF16)

    tm, half = ya_ref.shape
    sub = min(tm, OUT_SUB_ROWS)
    mod_rows = _mod_rows
    for c in range(tm // sub):
        rs = slice(c * sub, (c + 1) * sub)
        acc = jnp.dot(ya_ref[rs, :].astype(BF16), wbf_ref[0:half, :], preferred_element_type=F32)
        acc = acc + jnp.dot(yb_ref[rs, :].astype(BF16), wbf_ref[half:, :], preferred_element_type=F32)
        x = x_ref[rs, :] + mod_rows(gt_ref, rs) * acc
        r = lax.rsqrt(jnp.mean(x * x, axis=-1, keepdims=True) + EPS)
        if final:
            o_ref = rest[0]
            o_ref[rs, :] = (x * r) * g_ref[...]
        else:
            sc_ref, sh_ref, x_out_ref, h_ref = rest[:4]
            x_out_ref[rs, :] = x
            h = (x * r) * g_ref[...] * (1.0 + mod_rows(sc_ref, rs)) + mod_rows(sh_ref, rs)
            h_ref[rs, :] = h.astype(h_ref.dtype)


def _out_proj(ya, yb, ya_col, yb_col, w, x, mod, g, next_mod, tm):
    rows = x.shape[0]
    half = D_MODEL // 2
    final = next_mod is None
    row_f32 = pl.BlockSpec((tm, D_MODEL), lambda i: (i, 0))
    in_specs = [
        pl.BlockSpec((tm, half), lambda i: (i, ya_col)),
        pl.BlockSpec((tm, half), lambda i: (i, yb_col)),
        pl.BlockSpec((D_MODEL, D_MODEL), lambda i: (0, 0), pipeline_mode=pl.Buffered(1)),
        row_f32,
        mod.spec(2, tm),
        pl.BlockSpec((1, D_MODEL), lambda i: (0, 0)),
    ]
    args = [ya, yb, w, x, mod.arr, g.reshape(1, D_MODEL)]
    if final:
        out_specs, out_shape = row_f32, jax.ShapeDtypeStruct((rows, D_MODEL), F32)
    else:
        in_specs += [next_mod.spec(1, tm), next_mod.spec(0, tm)]
        args += [next_mod.arr, next_mod.arr]
        out_specs = [row_f32, row_f32]
        out_shape = [jax.ShapeDtypeStruct((rows, D_MODEL), F32), jax.ShapeDtypeStruct((rows, D_MODEL), BF16)]
    return pl.pallas_call(
        functools.partial(_out_proj_kernel, final=final),
        grid=(rows // tm,),
        in_specs=in_specs,
        out_specs=out_specs,
        out_shape=out_shape,
        scratch_shapes=[pltpu.VMEM((D_MODEL, D_MODEL), BF16)],
        compiler_params=_params("arbitrary"),
        name="out_proj_final" if final else "out_proj",
    )(*args)


def _conv_proj_kernel(h_ref, wa_ref, wb_ref, wz_ref, glu_ref, sz_ref, wbf_ref):
    @pl.when(pl.program_id(1) == 0)
    def _():
        wbf_ref[0] = wa_ref[...].astype(BF16)
        wbf_ref[1] = wb_ref[...].astype(BF16)
        wbf_ref[2] = wz_ref[...].astype(BF16)

    h = h_ref[...]
    a = jnp.dot(h, wbf_ref[0], preferred_element_type=F32)
    b = jnp.dot(h, wbf_ref[1], preferred_element_type=F32)
    glu_ref[...] = a * _sigmoid(b)
    sz_ref[...] = _silu(jnp.dot(h, wbf_ref[2], preferred_element_type=F32))


def _conv_proj(h, w, tm):
    rows = h.shape[0]
    tn = 512
    nj = D_C // tn
    out = pl.BlockSpec((tm, tn), lambda j, i: (i, j))
    return pl.pallas_call(
        _conv_proj_kernel,
        grid=(nj, rows // tm),
        in_specs=[
            pl.BlockSpec((tm, D_MODEL), lambda j, i: (i, 0)),
            pl.BlockSpec((D_MODEL, tn), lambda j, i: (0, j)),
            pl.BlockSpec((D_MODEL, tn), lambda j, i: (0, nj + j)),
            pl.BlockSpec((D_MODEL, tn), lambda j, i: (0, 2 * nj + j)),
        ],
        out_specs=[out, out],
        out_shape=[jax.ShapeDtypeStruct((rows, D_C), F32)] * 2,
        scratch_shapes=[pltpu.VMEM((3, D_MODEL, tn), BF16)],
        compiler_params=_params("arbitrary", "arbitrary"),
        name="conv_proj",
    )(h, w, w, w)


HALO = 32
CONV_LANES = 128
LN_ROWS = 16


def _conv_ln_kernel(glu_ref, halo_ref, sz_ref, w_ref, cb_ref, g_ref, b_ref, o_ref, buf_scr, y_scr, wb_scr,
                    *, tiles_per_seq, tm):
    first = (pl.program_id(0) % tiles_per_seq) == 0

    @pl.when(pl.program_id(0) == 0)
    def _():
        for k in range(CONV_W):
            wb_scr[k] = jnp.broadcast_to(w_ref[k:k + 1, :], (SUBLANES, D_C))

    @pl.when(first)
    def _():
        buf_scr[0:HALO, :] = jnp.zeros((HALO, D_C), F32)

    @pl.when(jnp.logical_not(first))
    def _():
        buf_scr[0:HALO, :] = halo_ref[...]

    buf_scr[HALO:HALO + tm, :] = glu_ref[...]

    off = HALO - (CONV_W - 1)
    taps = [[(a, SUBLANES * a + r - off) for a in range((CONV_W + off) // SUBLANES + 1)
             if 0 <= SUBLANES * a + r - off < CONV_W] for r in range(SUBLANES)]
    rowid = lax.broadcasted_iota(jnp.int32, (SUBLANES, CONV_LANES), 0)

    for lc in range(D_C // CONV_LANES):
        lanes = slice(lc * CONV_LANES, (lc + 1) * CONV_LANES)

        def z_block(u0, r):
            acc = None
            for a, k in taps[r]:
                term = wb_scr[k, :, lanes] * buf_scr[pl.ds(u0 + SUBLANES * a, SUBLANES), lanes]
                acc = term if acc is None else acc + term
            return acc

        def body(blk, carry):
            t0 = pl.multiple_of(blk * SUBLANES, SUBLANES)
            terms = [z_block(t0, 0)]
            nxt = []
            for r in range(1, SUBLANES):
                zn = z_block(t0 + SUBLANES, r)
                mixed = jnp.where(rowid >= r, carry[r - 1], zn)
                terms.append(pltpu.roll(mixed, SUBLANES - r, 0))
                nxt.append(zn)
            while len(terms) > 1:
                terms = [terms[i] + terms[i + 1] for i in range(0, len(terms), 2)]
            y_scr[pl.ds(t0, SUBLANES), lanes] = terms[0]
            return tuple(nxt)

        init = tuple(z_block(0, r) for r in range(1, SUBLANES))
        lax.fori_loop(0, tm // SUBLANES, body, init, unroll=4)

    def norm_rows(blk, carry):
        r0 = pl.multiple_of(blk * LN_ROWS, LN_ROWS)
        rs = pl.ds(r0, LN_ROWS)
        y = _layernorm_rows(y_scr[rs, :] + cb_ref[...], g_ref[...], b_ref[...])
        o_ref[rs, :] = (_silu(y) * sz_ref[rs, :]).astype(o_ref.dtype)
        return carry

    lax.fori_loop(0, tm // LN_ROWS, norm_rows, 0, unroll=8)


def _conv_ln(glu, sz, conv_w, conv_b, ln_g, ln_b, seq, tm):
    rows = glu.shape[0]
    vec = pl.BlockSpec((1, D_C), lambda i: (0, 0))
    tile = pl.BlockSpec((tm, D_C), lambda i: (i, 0))
    return pl.pallas_call(
        functools.partial(_conv_ln_kernel, tiles_per_seq=seq // tm, tm=tm),
        grid=(rows // tm,),
        in_specs=[tile,
                  pl.BlockSpec((HALO, D_C), lambda i: (jnp.maximum(i * (tm // HALO) - 1, 0), 0)),
                  tile,
                  pl.BlockSpec((CONV_W, D_C), lambda i: (0, 0)),
                  vec, vec, vec],
        out_specs=tile,
        out_shape=jax.ShapeDtypeStruct((rows, D_C), BF16),
        scratch_shapes=[pltpu.VMEM((HALO + tm, D_C), F32), pltpu.VMEM((tm, D_C), F32),
                        pltpu.VMEM((CONV_W, SUBLANES, D_C), F32)],
        compiler_params=_params("arbitrary"),
        name="conv_ln",
    )(glu, glu, sz, conv_w, conv_b.reshape(1, D_C), ln_g.reshape(1, D_C), ln_b.reshape(1, D_C))


def _conv_ln_sample_kernel(st_ref, glu_ref, sz_ref, w_ref, cb_ref, g_ref, b_ref, o_ref, ns_ref, buf_scr,
                           *, n_new):
    hist = CONV_W - 1
    buf_scr[0:hist, :] = st_ref[0]
    buf_scr[hist:hist + n_new, :] = glu_ref[0]
    acc = jnp.zeros((n_new, D_C), F32) + cb_ref[...]
    for k in range(CONV_W):
        acc = acc + w_ref[k:k + 1, :] * buf_scr[k:k + n_new, :]
    y = _layernorm_rows(acc, g_ref[...], b_ref[...])
    o_ref[0] = _silu(y) * sz_ref[0]
    ns_ref[0] = buf_scr[n_new:n_new + hist, :]


def _conv_ln_sample(state, glu, sz, conv_w, conv_b, ln_g, ln_b, n_seq, n_new):
    hist = CONV_W - 1
    vec = pl.BlockSpec((1, D_C), lambda s: (0, 0))
    new = pl.BlockSpec((1, n_new, D_C), lambda s: (s, 0, 0))
    st = pl.BlockSpec((1, hist, D_C), lambda s: (s, 0, 0))
    y, ns = pl.pallas_call(
        functools.partial(_conv_ln_sample_kernel, n_new=n_new),
        grid=(n_seq,),
        in_specs=[st, new, new, pl.BlockSpec((CONV_W, D_C), lambda s: (0, 0)), vec, vec, vec],
        out_specs=[new, st],
        out_shape=[jax.ShapeDtypeStruct((n_seq, n_new, D_C), F32),
                   jax.ShapeDtypeStruct((n_seq, hist, D_C), F32)],
        scratch_shapes=[pltpu.VMEM((hist + n_new, D_C), F32)],
        compiler_params=_params("arbitrary"),
        name="conv_ln_sample",
    )(state, glu.reshape(n_seq, n_new, D_C), sz.reshape(n_seq, n_new, D_C), conv_w,
      conv_b.reshape(1, D_C), ln_g.reshape(1, D_C), ln_b.reshape(1, D_C))
    return y.reshape(n_seq * n_new, D_C), ns


def _trunk(x, mods, tm, seq, weights, attend, conv):
    (norm_g, ab_w_in, a_ln_g, a_ln_b, a_ws, a_bs, ab_w_out, c_w_in, c_w_out, final_norm_g) = weights
    tm_row = min(tm, ROW_TILE)
    u, h = _proj_norm(x, norm_g[0], mods[0], ab_w_in[0], tm)
    seg = lambda s, ep: _proj_seg(h, ab_w_in[0], s, ep, a_ln_g[0], a_ln_b[0], tm)
    vn = seg(1, "gelu_ln")
    q, zb = seg(3, "none"), seg(6, "silu")
    (k, k_heads), (v, v_heads) = seg(4, "heads"), seg(5, "heads")
    ya = _a_mix(h, ab_w_in[0], 2, u, vn, a_ws[0], a_bs[0], seq, tm)
    yb = attend(q, k, v, zb)
    x1, h1 = _out_proj(ya, yb, 0, 0, ab_w_out[0], x, mods[0], norm_g[1], mods[1], tm_row)
    glu, sz = _conv_proj(h1, c_w_in[0], tm)
    y2, conv_state = conv(glu, sz)
    y = _out_proj(y2, y2, 0, 1, c_w_out[0], x1, mods[1], final_norm_g, None, tm_row)
    return y, k_heads, v_heads, vn, conv_state


def kernel(x_prompt, x_sample, cache_b_k, cache_b_v, state_c_conv, page_table, c_prompt, c_sample,
           ada_w, ada_b, norm_g, ab_w_in, a_ln_g, a_ln_b, a_ws, a_bs, ab_w_out,
           c_w_in, c_conv_w, c_conv_b, c_ln_g, c_ln_b, c_w_out, final_norm_g):
    batch, seq, _ = x_prompt.shape
    n_seq, n_new, _ = x_sample.shape
    weights = (norm_g, ab_w_in, a_ln_g, a_ln_b, a_ws, a_bs, ab_w_out, c_w_in, c_w_out, final_norm_g)

    mod = _adaln(jnp.concatenate([c_prompt, c_sample], axis=0), ada_w, ada_b)
    mod = mod.reshape(DEPTH, batch + n_seq, 3, D_MODEL)
    tm_p = PROJ_TILE
    mods_p, mods_s = [], []
    for l in range(DEPTH):
        mods_p.append(_Mod(mod[l, :batch].reshape(batch * 3, 1, D_MODEL), seq))
        per_row = jnp.repeat(mod[l, batch:], n_new, axis=0).transpose(1, 0, 2)
        mods_s.append(_Mod(per_row))

    yp, kp, vp, _, glu_p = _trunk(
        x_prompt.reshape(batch * seq, D_MODEL), mods_p, tm_p, seq, weights,
        attend=lambda q, k, v, zb: _moba_prompt(q, k, v, zb, batch, seq),
        conv=lambda glu, sz: (_conv_ln(glu, sz, c_conv_w[0], c_conv_b[0], c_ln_g[0], c_ln_b[0], seq, ROW_TILE), glu))
    ys, ks, vs, vas, conv_s = _trunk(
        x_sample.reshape(n_seq * n_new, D_MODEL), mods_s, n_seq * n_new, n_new, weights,
        attend=lambda q, k, v, zb: _moba_sample(q, k, v, zb, cache_b_k[0], cache_b_v[0], page_table,
                                               n_seq, n_new),
        conv=lambda glu, sz: _conv_ln_sample(state_c_conv[0], glu, sz, c_conv_w[0], c_conv_b[0],
                                             c_ln_g[0], c_ln_b[0], n_seq, n_new))

    hd = (B_HEADS, B_HEAD_DIM)
    conv_p = glu_p.reshape(batch, seq, D_C)[:, seq - (CONV_W - 1):, :]
    return (yp.reshape(batch, seq, D_MODEL), ys.reshape(n_seq, n_new, D_MODEL),
            kp.reshape(1, batch, seq, *hd), vp.reshape(1, batch, seq, *hd),
            ks.reshape(1, n_seq, n_new, *hd), vs.reshape(1, n_seq, n_new, *hd),
            vas.reshape(1, n_seq, n_new, D_A), conv_p[None], conv_s[None])
```

```python
import functools

import numpy as np
import jax
import jax.numpy as jnp
from jax import lax
from jax.experimental import pallas as pl
from jax.experimental.pallas import tpu as pltpu

F32 = jnp.float32
BF16 = jnp.bfloat16
HIGHEST = lax.Precision.HIGHEST

D_MODEL = 2048
DEPTH = 2
PAST_LEN = 8192
PAGE_SIZE = 128
D_A = 1024
A_GROUPS = 8
A_CHUNK = 128
B_HEADS = 8
B_HEAD_DIM = 128
D_B = B_HEADS * B_HEAD_DIM
MOBA_BLOCK = 256
MOBA_TOPK = 3
D_C = D_MODEL
CONV_W = 31
EPS = 1e-6
NEG = -1e30
QK_SCALE = B_HEAD_DIM ** -0.5

VMEM_LIMIT_BYTES = 56 * 1024 * 1024
SUBLANES = 8
LANES = 128
PROJ_TILE = 1024
ROW_TILE = 512
OUT_SUB_ROWS = 256

NT_DIMS = (((1,), (1,)), ((), ()))


def _params(*sem):
    return pltpu.CompilerParams(dimension_semantics=sem, vmem_limit_bytes=VMEM_LIMIT_BYTES)


def _div_pow2(x, d):
    shift = int(d).bit_length() - 1
    assert 1 << shift == d
    return lax.shift_right_logical(x, jnp.int32(shift))


def _sigmoid(x):
    return jax.nn.sigmoid(x)


def _silu(x):
    return x * _sigmoid(x)


def _gelu_tanh(x):
    c = np.float32(np.sqrt(2.0 / np.pi))
    return x * (0.5 * (1.0 + jnp.tanh(c * (x + 0.044715 * (x * x * x)))))


def _layernorm_rows(x, g, b):
    mu = jnp.mean(x, axis=-1, keepdims=True)
    xc = x - mu
    var = jnp.mean(xc * xc, axis=-1, keepdims=True)
    return (xc * lax.rsqrt(var + EPS)) * g + b


def _adaln_kernel(c_ref, w_ref, b_ref, o_ref):
    s = _silu(c_ref[...]).astype(BF16)
    o_ref[0] = jnp.dot(s, w_ref[0].astype(BF16), preferred_element_type=F32) + b_ref[0]


def _adaln(c_all, ada_w, ada_b):
    n = c_all.shape[0]
    tn = 1024
    return pl.pallas_call(
        _adaln_kernel,
        grid=(DEPTH, 3 * D_MODEL // tn),
        in_specs=[
            pl.BlockSpec((n, D_MODEL), lambda l, j: (0, 0)),
            pl.BlockSpec((1, D_MODEL, tn), lambda l, j: (l, 0, j)),
            pl.BlockSpec((1, 1, tn), lambda l, j: (l, 0, j)),
        ],
        out_specs=pl.BlockSpec((1, n, tn), lambda l, j: (l, 0, j)),
        out_shape=jax.ShapeDtypeStruct((DEPTH, n, 3 * D_MODEL), F32),
        compiler_params=_params("arbitrary", "arbitrary"),
        name="adaln",
    )(c_all, ada_w, ada_b.reshape(DEPTH, 1, 3 * D_MODEL))


class _Mod:
    def __init__(self, arr, seq_len=None):
        self.arr = arr
        self.rows = arr.shape[1]
        self.seq_len = seq_len

    def spec(self, which, tm):
        if self.seq_len is None:
            return pl.BlockSpec((1, self.rows, D_MODEL), lambda i: (which, 0, 0))
        assert self.seq_len % tm == 0
        tiles = self.seq_len // tm
        return pl.BlockSpec((1, 1, D_MODEL), lambda i: ((i // tiles) * 3 + which, 0, 0))


def _mod_rows(ref, rs):
    return ref[0] if ref.shape[1] == 1 else ref[0, rs, :]


def _proj_norm_kernel(x_ref, ng_ref, sc_ref, sh_ref, w_ref, u_ref, h_ref, wbf_ref):
    @pl.when(pl.program_id(0) == 0)
    def _():
        wbf_ref[...] = w_ref[...].astype(BF16)

    tm = x_ref.shape[0]
    sub = min(tm, OUT_SUB_ROWS)
    for c in range(tm // sub):
        rs = slice(c * sub, (c + 1) * sub)
        x = x_ref[rs, :]
        r = lax.rsqrt(jnp.mean(x * x, axis=-1, keepdims=True) + EPS)
        h = ((x * r) * ng_ref[...] * (1.0 + _mod_rows(sc_ref, rs)) + _mod_rows(sh_ref, rs)).astype(BF16)
        h_ref[rs, :] = h
        u_ref[rs, :] = _gelu_tanh(jnp.dot(h, wbf_ref[...], preferred_element_type=F32))


def _proj_norm(x, g, mod, w, tm):
    rows = x.shape[0]
    tn = 1024
    return pl.pallas_call(
        _proj_norm_kernel,
        grid=(rows // tm,),
        in_specs=[
            pl.BlockSpec((tm, D_MODEL), lambda i: (i, 0)),
            pl.BlockSpec((1, D_MODEL), lambda i: (0, 0)),
            mod.spec(1, tm),
            mod.spec(0, tm),
            pl.BlockSpec((D_MODEL, tn), lambda i: (0, 0), pipeline_mode=pl.Buffered(1)),
        ],
        out_specs=[pl.BlockSpec((tm, tn), lambda i: (i, 0)), pl.BlockSpec((tm, D_MODEL), lambda i: (i, 0))],
        out_shape=[jax.ShapeDtypeStruct((rows, tn), F32), jax.ShapeDtypeStruct((rows, D_MODEL), BF16)],
        scratch_shapes=[pltpu.VMEM((D_MODEL, tn), BF16)],
        compiler_params=_params("arbitrary"),
        name="proj_norm_gelu",
    )(x, g.reshape(1, D_MODEL), mod.arr, mod.arr, w)


def _proj_seg_kernel(h_ref, w_ref, g_ref, b_ref, *rest, epilogue):
    o_ref, wbf_ref = rest[0], rest[-1]

    @pl.when(pl.program_id(0) == 0)
    def _():
        wbf_ref[...] = w_ref[...].astype(BF16)

    acc = jnp.dot(h_ref[...], wbf_ref[...], preferred_element_type=F32)
    if epilogue == "gelu_ln":
        acc = _layernorm_rows(_gelu_tanh(acc), g_ref[...], b_ref[...])
    elif epilogue == "silu":
        acc = _silu(acc)
    o_ref[...] = acc
    if epilogue == "heads":
        oh_ref = rest[1]
        tm = acc.shape[0]
        for hd in range(B_HEADS):
            oh_ref[pl.ds(hd, tm, stride=B_HEADS), :] = acc[:, hd * B_HEAD_DIM:(hd + 1) * B_HEAD_DIM]


def _proj_seg(h, w, seg, epilogue, ln_g, ln_b, tm):
    rows = h.shape[0]
    tn = 1024
    out_specs = [pl.BlockSpec((tm, tn), lambda i: (i, 0))]
    out_shape = [jax.ShapeDtypeStruct((rows, tn), F32)]
    if epilogue == "heads":
        out_specs.append(pl.BlockSpec((tm * B_HEADS, B_HEAD_DIM), lambda i: (i, 0)))
        out_shape.append(jax.ShapeDtypeStruct((rows * B_HEADS, B_HEAD_DIM), F32))
    out = pl.pallas_call(
        functools.partial(_proj_seg_kernel, epilogue=epilogue),
        grid=(rows // tm,),
        in_specs=[
            pl.BlockSpec((tm, D_MODEL), lambda i: (i, 0)),
            pl.BlockSpec((D_MODEL, tn), lambda i: (0, seg), pipeline_mode=pl.Buffered(1)),
            pl.BlockSpec((1, tn), lambda i: (0, 0)),
            pl.BlockSpec((1, tn), lambda i: (0, 0)),
        ],
        out_specs=out_specs,
        out_shape=out_shape,
        scratch_shapes=[pltpu.VMEM((D_MODEL, tn), BF16)],
        compiler_params=_params("arbitrary"),
        name="proj_" + epilogue,
    )(h, w, ln_g.reshape(1, tn), ln_b.reshape(1, tn))
    return out if epilogue == "heads" else out[0]


def _a_mix_kernel(h_ref, wz_ref, u_ref, v_ref, w_ref, bst_ref, o_ref, wbf_ref, z_ref, *, seq_len):
    @pl.when(pl.program_id(0) == 0)
    def _():
        wbf_ref[...] = wz_ref[...].astype(BF16)

    z_ref[...] = _silu(jnp.dot(h_ref[...], wbf_ref[...], preferred_element_type=F32))
    tm = h_ref.shape[0]
    row = lax.broadcasted_iota(jnp.int32, (A_CHUNK, A_CHUNK), 0)
    col = lax.broadcasted_iota(jnp.int32, (A_CHUNK, A_CHUNK), 1)
    keep = jnp.where(col <= row, 1.0, 0.0)
    if seq_len < A_CHUNK:
        keep = keep * jnp.where(_div_pow2(row, seq_len) == _div_pow2(col, seq_len), 1.0, 0.0)
    gw = D_A // A_GROUPS
    for g in range(A_GROUPS):
        wg = (w_ref[g] * keep).astype(BF16)
        bias = bst_ref[:, g:g + 1]
        for c in range(tm // A_CHUNK):
            rs = slice(c * A_CHUNK, (c + 1) * A_CHUNK)
            cs = slice(g * gw, (g + 1) * gw)
            mix = jnp.dot(wg, v_ref[rs, cs].astype(BF16), preferred_element_type=F32) + bias
            o_ref[rs, cs] = ((u_ref[rs, cs] * mix) * z_ref[rs, cs]).astype(o_ref.dtype)


def _a_mix(h, w, seg, u, vn, ws, bs, seq_len, tm):
    rows = u.shape[0]
    n = min(seq_len, A_CHUNK)
    reps = A_CHUNK // n
    wmat = jnp.tile(ws[:, :n, :n], (1, reps, reps))
    bst = jnp.tile(bs[:, :n], (1, reps)).T
    spec = pl.BlockSpec((tm, D_A), lambda i: (i, 0))
    return pl.pallas_call(
        functools.partial(_a_mix_kernel, seq_len=n),
        grid=(rows // tm,),
        in_specs=[pl.BlockSpec((tm, D_MODEL), lambda i: (i, 0)),
                  pl.BlockSpec((D_MODEL, D_A), lambda i: (0, seg), pipeline_mode=pl.Buffered(1)),
                  spec, spec,
                  pl.BlockSpec((A_GROUPS, A_CHUNK, A_CHUNK), lambda i: (0, 0, 0)),
                  pl.BlockSpec((A_CHUNK, A_GROUPS), lambda i: (0, 0))],
        out_specs=spec,
        out_shape=jax.ShapeDtypeStruct((rows, D_A), BF16),
        scratch_shapes=[pltpu.VMEM((D_MODEL, D_A), BF16), pltpu.VMEM((tm, D_A), F32)],
        compiler_params=_params("arbitrary"),
        name="proj_silu_a_mix",
    )(h, w, u, vn, wmat, bst)


def _rank_select(g, valid, index, axis, count):
    gm = jnp.where(valid, g, NEG)
    n = g.shape[axis]
    rank = jnp.zeros(g.shape, F32)
    for m in range(n):
        ref = gm[m:m + 1, :] if axis == 0 else gm[:, m:m + 1]
        beats = jnp.where(ref > gm, 1.0, jnp.where(ref == gm, jnp.where(m < index, 1.0, 0.0), 0.0))
        rank = rank + beats
    return jnp.where(valid, jnp.where(rank < float(count), 1.0, 0.0), 0.0), rank


POS_SPLIT = 16


def _moba_prompt_kernel(slope_ref, q_ref, k_ref, v_ref, z_ref, o_ref, qa_scr, ka_scr, va_scr, s_scr, p_scr,
                        *, seq):
    nb = seq // MOBA_BLOCK
    assert nb + 2 <= LANES and seq // POS_SPLIT <= 256
    slope = slope_ref[pl.program_id(1)]

    @pl.when((pl.program_id(0) == 0) & (pl.program_id(1) == 0))
    def _():
        pos = lax.broadcasted_iota(jnp.int32, (seq, LANES), 0)
        lane = lax.broadcasted_iota(jnp.int32, (seq, LANES), 1)
        onehot = jnp.where(lane == _div_pow2(pos, MOBA_BLOCK), 1.0, 0.0)
        hi = _div_pow2(pos, POS_SPLIT).astype(F32)
        lo = (pos - _div_pow2(pos, POS_SPLIT) * POS_SPLIT).astype(F32)
        aug = jnp.where(lane < nb, onehot, jnp.where(lane == nb, hi, jnp.where(lane == nb + 1, lo, 0.0)))
        ka_scr[:, B_HEAD_DIM:] = aug.astype(BF16)
        va_scr[:, B_HEAD_DIM:] = jnp.ones((seq, LANES), BF16)

    k = k_ref[0]
    ka_scr[:, 0:B_HEAD_DIM] = k.astype(BF16)
    va_scr[:, 0:B_HEAD_DIM] = v_ref[0].astype(BF16)

    kbar = jnp.mean(k.reshape(nb, MOBA_BLOCK, B_HEAD_DIM), axis=1)
    qf = q_ref[0] * QK_SCALE
    gt = lax.dot_general(kbar, qf, NT_DIMS, precision=HIGHEST, preferred_element_type=F32)
    blk = lax.broadcasted_iota(jnp.int32, (nb, seq), 0)
    own = _div_pow2(lax.broadcasted_iota(jnp.int32, (nb, seq), 1), MOBA_BLOCK)
    sel, _ = _rank_select(gt, blk < own, blk, 0, MOBA_TOPK)
    pen = jnp.where(sel > 0.5, 0.0, jnp.where(blk == own, 0.0, NEG))
    pen = jnp.concatenate([pen, jnp.zeros((LANES - nb, seq), F32)], axis=0)
    row = lax.broadcasted_iota(jnp.int32, (LANES, seq), 0)
    aug_t = jnp.where(row == nb, POS_SPLIT * slope, jnp.where(row == nb + 1, slope, pen))
    qa_scr[:, B_HEAD_DIM:] = aug_t.T.astype(BF16)
    qa_scr[:, 0:B_HEAD_DIM] = qf.astype(BF16)

    causal = (lax.broadcasted_iota(jnp.int32, (MOBA_BLOCK, MOBA_BLOCK), 0)
              >= lax.broadcasted_iota(jnp.int32, (MOBA_BLOCK, MOBA_BLOCK), 1))
    offs = [MOBA_BLOCK * (jb * (jb + 1) // 2) for jb in range(nb + 1)]

    def scores(jb):
        rows = slice(jb * MOBA_BLOCK, (jb + 1) * MOBA_BLOCK)
        nk = (jb + 1) * MOBA_BLOCK
        s_scr[:, offs[jb]:offs[jb + 1]] = lax.dot_general(
            qa_scr[rows, :], ka_scr[0:nk, :], NT_DIMS, preferred_element_type=F32)

    def weights(jb):
        n_past = jb * MOBA_BLOCK
        own_cols = slice(offs[jb] + n_past, offs[jb + 1])
        own_s = jnp.where(causal, s_scr[:, own_cols], NEG)
        m = jnp.max(own_s, axis=-1, keepdims=True)
        if jb > 0:
            past_cols = slice(offs[jb], offs[jb] + n_past)
            m = jnp.maximum(m, jnp.max(s_scr[:, past_cols], axis=-1, keepdims=True))
            p_scr[:, past_cols] = jnp.exp(s_scr[:, past_cols] - m).astype(BF16)
        p_scr[:, own_cols] = jnp.exp(own_s - m).astype(BF16)

    def outputs(jb):
        rows = slice(jb * MOBA_BLOCK, (jb + 1) * MOBA_BLOCK)
        nk = (jb + 1) * MOBA_BLOCK
        o = jnp.dot(p_scr[:, offs[jb]:offs[jb + 1]], va_scr[0:nk, :], preferred_element_type=F32)
        y = o[:, 0:B_HEAD_DIM] / o[:, B_HEAD_DIM:B_HEAD_DIM + 1]
        o_ref[0, rows, :] = (y * z_ref[0, rows, :]).astype(o_ref.dtype)

    for jb in range(nb):
        scores(jb)
    for jb in range(nb + 1):
        if jb < nb:
            weights(jb)
        if jb >= 1:
            outputs(jb - 1)


def _alibi_slopes():
    return jnp.asarray(np.array([2.0 ** (-8.0 * (h + 1) / B_HEADS) for h in range(B_HEADS)], np.float32))


def _moba_prompt(q, k, v, zs, batch, seq):
    shp = (batch, seq, D_B)
    spec = pl.BlockSpec((1, seq, B_HEAD_DIM), lambda b, h: (b, 0, h))
    nb = seq // MOBA_BLOCK
    n_score_cols = MOBA_BLOCK * (nb * (nb + 1) // 2)
    out = pl.pallas_call(
        functools.partial(_moba_prompt_kernel, seq=seq),
        grid=(batch, B_HEADS),
        in_specs=[pl.BlockSpec(memory_space=pltpu.SMEM), spec, spec, spec, spec],
        out_specs=spec,
        out_shape=jax.ShapeDtypeStruct(shp, BF16),
        scratch_shapes=[pltpu.VMEM((seq, B_HEAD_DIM + LANES), BF16)] * 3
                       + [pltpu.VMEM((MOBA_BLOCK, n_score_cols), F32),
                          pltpu.VMEM((MOBA_BLOCK, n_score_cols), BF16)],
        compiler_params=_params("arbitrary", "arbitrary"),
        name="moba_prompt",
    )(_alibi_slopes(), q.reshape(shp), k.reshape(shp), v.reshape(shp), zs.reshape(shp))
    return out.reshape(batch * seq, D_B)


CHUNK_PAGES = 8
K_RING = 7


def _moba_sample_kernel(pt_ref, slope_ref, trow_ref, q_ref, kn_ref, vn_ref, z_ref, ck_ref, cv_ref, o_ref,
                        kbuf, vbuf, ksem, vsem, isem, idx_v, idx_s,
                        qbd_scr, qbf_scr, cat_scr, s_scr, p_scr, ksum_scr, onew_scr, l_scr,
                        *, n_seq, n_new, n_pages):
    pg = CHUNK_PAGES
    nch = n_pages // pg
    total = n_seq * nch
    n_blocks = n_pages * PAGE_SIZE // MOBA_BLOCK
    blocks_per_chunk = pg * PAGE_SIZE // MOBA_BLOCK
    pages_per_block = MOBA_BLOCK // PAGE_SIZE
    rows = n_new * B_HEADS
    t = pl.program_id(0)
    head_of_lane = _div_pow2(lax.broadcasted_iota(jnp.int32, (B_HEADS, D_B), 1), B_HEAD_DIM)
    diag = head_of_lane == lax.broadcasted_iota(jnp.int32, (B_HEADS, D_B), 0)
    slope = slope_ref[:, 0:1]
    trow = trow_ref[:, 0:1]

    def k_copy(page, slot, r):
        return pltpu.make_async_copy(ck_ref.at[page], kbuf.at[slot, r], ksem.at[slot])

    def start_chunk(seq, ch, slot):
        for r in range(pg):
            k_copy(pt_ref[seq, ch * pg + r], slot, r).start()

    def v_copy(page, r, j, half):
        dst = vbuf.at[r, pl.ds((j * pages_per_block + half) * PAGE_SIZE, PAGE_SIZE), :]
        return pltpu.make_async_copy(cv_ref.at[page, :, r % B_HEADS, :], dst, vsem.at[0])

    @pl.when(t == 0)
    def _():
        for f0 in range(K_RING - 1):
            start_chunk(0, f0, f0)

    @pl.when(t < n_seq)
    def _():
        tiles = []
        for qi in range(n_new):
            qrow = q_ref[0, qi:qi + 1, :] * QK_SCALE
            tiles.append(jnp.where(diag, jnp.broadcast_to(qrow, (B_HEADS, D_B)), 0.0))
        qbd = jnp.concatenate(tiles, axis=0)
        qbf_scr[...] = qbd
        qbd_scr[...] = qbd.astype(BF16)

        def chunk_body(c, carry):
            f = t * nch + c
            ahead = f + (K_RING - 1)

            @pl.when(ahead < total)
            def _():
                start_chunk(lax.div(ahead, jnp.int32(nch)), lax.rem(ahead, jnp.int32(nch)),
                            lax.rem(ahead, jnp.int32(K_RING)))

            slot = lax.rem(f, jnp.int32(K_RING))
            for r in range(pg):
                k_copy(0, slot, r).wait()
            pair = None
            for r in range(pg):
                for hd in range(B_HEADS):
                    part = kbuf[slot, r, pl.ds(hd, PAGE_SIZE, stride=B_HEADS), :]
                    cat_scr[r * PAGE_SIZE:(r + 1) * PAGE_SIZE,
                            hd * B_HEAD_DIM:(hd + 1) * B_HEAD_DIM] = part.astype(BF16)
                ps = jnp.sum(kbuf[slot, r].reshape(PAGE_SIZE, B_HEADS, B_HEAD_DIM), axis=0)
                if r % pages_per_block == 0:
                    pair = ps
                else:
                    pair = pair + ps
                if r % pages_per_block == pages_per_block - 1:
                    ksum_scr[c * blocks_per_chunk + r // pages_per_block] = pair
            s_scr[c] = lax.dot_general(qbd_scr[...], cat_scr[...], NT_DIMS, preferred_element_type=F32)
            return carry

        lax.fori_loop(0, nch, chunk_body, 0)

    @pl.when(t >= 1)
    def _():
        for r in range(rows):
            for j in range(MOBA_TOPK):
                for half in range(pages_per_block):
                    v_copy(0, r, j, half).wait()
        head_rows = []
        for r in range(rows):
            parts = [p_scr[idx_s[r, j], r:r + 1, :] for j in range(MOBA_TOPK)]
            p_r = jnp.broadcast_to(jnp.concatenate(parts, axis=1), (SUBLANES, MOBA_TOPK * MOBA_BLOCK))
            o = jnp.dot(p_r.astype(BF16), vbuf[r].astype(BF16), preferred_element_type=F32)
            head_rows.append(o[0:1, :])
        out_rows = []
        for qi in range(n_new):
            sl = slice(qi * B_HEADS, (qi + 1) * B_HEADS)
            o_sel = jnp.concatenate(head_rows[sl], axis=1)
            o_new = jnp.sum(jnp.where(diag, onew_scr[sl, :], 0.0), axis=0, keepdims=True)
            l_row = jnp.concatenate([l_scr[r:r + 1, :] for r in range(sl.start, sl.stop)], axis=1)
            out_rows.append((o_sel + o_new) / l_row)
        o_ref[0] = jnp.concatenate(out_rows, axis=0) * z_ref[0]

    @pl.when(t < n_seq)
    def _():
        kbar = jnp.concatenate([ksum_scr[:, hd, :] for hd in range(B_HEADS)], axis=1)
        kbar = kbar * (1.0 / MOBA_BLOCK)
        g = lax.dot_general(qbf_scr[...], kbar, NT_DIMS, precision=HIGHEST,
                            preferred_element_type=F32)
        blk = lax.broadcasted_iota(jnp.int32, (rows, n_blocks), 1)
        sel, rank = _rank_select(g, blk >= 0, blk, 1, MOBA_TOPK)

        lane = lax.broadcasted_iota(jnp.int32, (rows, LANES), 1)
        ids = jnp.zeros((rows, LANES), F32)
        for j in range(MOBA_TOPK):
            col = jnp.sum(jnp.where(rank == float(j), blk.astype(F32), 0.0), axis=-1, keepdims=True)
            ids = jnp.where(lane == j, col, ids)
        idx_v[...] = ids.astype(jnp.int32)
        to_smem = pltpu.make_async_copy(idx_v, idx_s, isem.at[0])
        to_smem.start()
        to_smem.wait()
        for r in range(rows):
            for j in range(MOBA_TOPK):
                for half in range(pages_per_block):
                    v_copy(pt_ref[t, idx_s[r, j] * pages_per_block + half], r, j, half).start()

        key = lax.broadcasted_iota(jnp.int32, (rows, MOBA_BLOCK), 1).astype(F32)
        top = None
        for n in range(n_blocks):
            gi, a = divmod(n, blocks_per_chunk)
            dist = trow - (float(n * MOBA_BLOCK) + key)
            slab = s_scr[gi, :, a * MOBA_BLOCK:(a + 1) * MOBA_BLOCK] - slope * dist
            slab = jnp.where(sel[:, n:n + 1] > 0.5, slab, NEG)
            p_scr[n] = slab
            top = slab if top is None else jnp.maximum(top, slab)
        m = jnp.max(top, axis=-1, keepdims=True)
        knew = jnp.concatenate([kn_ref[0], jnp.zeros((LANES - n_new, D_B), F32)], axis=0).astype(BF16)
        s_new = lax.dot_general(qbd_scr[...], knew, NT_DIMS, preferred_element_type=F32)
        dist_new = trow - (float(n_pages * PAGE_SIZE) + lane.astype(F32))
        s_new = jnp.where(dist_new >= 0.0, s_new - slope * dist_new, NEG)
        m = jnp.maximum(m, jnp.max(s_new, axis=-1, keepdims=True))
        tot = jnp.zeros((rows, MOBA_BLOCK), F32)
        for n in range(n_blocks):
            p = jnp.exp(p_scr[n] - m)
            tot = tot + p
            p_scr[n] = p
        pn = jnp.exp(s_new - m)
        l = jnp.sum(tot, axis=-1, keepdims=True) + jnp.sum(pn, axis=-1, keepdims=True)
        l_scr[...] = jnp.broadcast_to(l, (rows, LANES))
        vnew = jnp.concatenate([vn_ref[0], jnp.zeros((LANES - n_new, D_B), F32)], axis=0).astype(BF16)
        onew_scr[...] = jnp.dot(pn.astype(BF16), vnew, preferred_element_type=F32)


def _moba_sample(q, kn, vn, zs, cache_k, cache_v, page_table, n_seq, n_new):
    n_pages = page_table.shape[1]
    n_phys = cache_k.shape[0]
    pg = CHUNK_PAGES
    nch = n_pages // pg
    assert n_pages % pg == 0 and K_RING - 1 <= nch
    rows = n_new * B_HEADS
    past = n_pages * PAGE_SIZE
    shp = (n_seq, n_new, D_B)
    slope_rows = np.tile(np.array([2.0 ** (-8.0 * (h + 1) / B_HEADS) for h in range(B_HEADS)], np.float32),
                         n_new)
    t_rows = np.repeat(past + np.arange(n_new, dtype=np.float32), B_HEADS)
    slope_rows = jnp.asarray(np.tile(slope_rows[:, None], (1, LANES)))
    t_rows = jnp.asarray(np.tile(t_rows[:, None], (1, LANES)))

    const = pl.BlockSpec((rows, LANES), lambda t, pt: (0, 0))
    cur = pl.BlockSpec((1, n_new, D_B), lambda t, pt: (jnp.minimum(t, n_seq - 1), 0, 0))
    prev = pl.BlockSpec((1, n_new, D_B), lambda t, pt: (jnp.maximum(t - 1, 0), 0, 0))
    hbm = pl.BlockSpec(memory_space=pl.ANY)

    page_rows = PAGE_SIZE * B_HEADS
    n_blocks = past // MOBA_BLOCK
    out = pl.pallas_call(
        functools.partial(_moba_sample_kernel, n_seq=n_seq, n_new=n_new, n_pages=n_pages),
        grid_spec=pltpu.PrefetchScalarGridSpec(
            num_scalar_prefetch=1,
            grid=(n_seq + 1,),
            in_specs=[const, const, cur, cur, cur, prev, hbm, hbm],
            out_specs=prev,
            scratch_shapes=[
                pltpu.VMEM((K_RING, pg, page_rows, B_HEAD_DIM), F32),
                pltpu.VMEM((rows, MOBA_TOPK * MOBA_BLOCK, B_HEAD_DIM), F32),
                pltpu.SemaphoreType.DMA((K_RING,)),
                pltpu.SemaphoreType.DMA((1,)),
                pltpu.SemaphoreType.DMA((1,)),
                pltpu.VMEM((rows, LANES), jnp.int32),
                pltpu.SMEM((rows, LANES), jnp.int32),
                pltpu.VMEM((rows, D_B), BF16),
                pltpu.VMEM((rows, D_B), F32),
                pltpu.VMEM((pg * PAGE_SIZE, D_B), BF16),
                pltpu.VMEM((nch, rows, pg * PAGE_SIZE), F32),
                pltpu.VMEM((n_blocks, rows, MOBA_BLOCK), F32),
                pltpu.VMEM((n_blocks, B_HEADS, B_HEAD_DIM), F32),
                pltpu.VMEM((rows, D_B), F32),
                pltpu.VMEM((rows, LANES), F32),
            ]),
        out_shape=jax.ShapeDtypeStruct(shp, F32),
        compiler_params=_params("arbitrary"),
        name="moba_sample",
    )(page_table, slope_rows, t_rows, q.reshape(shp), kn.reshape(shp), vn.reshape(shp), zs.reshape(shp),
      cache_k.reshape(n_phys, page_rows, B_HEAD_DIM), cache_v)
    return out.reshape(n_seq * n_new, D_B)


def _out_proj_kernel(ya_ref, yb_ref, w_ref, x_ref, gt_ref, g_ref, *rest, final):
    wbf_ref = rest[-1]

    @pl.when(pl.program_id(0) == 0)
    def _():
        wbf_ref[...] = w_ref[...].astype(BF16)

    tm, half = ya_ref.shape
    sub = min(tm, OUT_SUB_ROWS)
    mod_rows = _mod_rows
    for c in range(tm // sub):
        rs = slice(c * sub, (c + 1) * sub)
        acc = jnp.dot(ya_ref[rs, :].astype(BF16), wbf_ref[0:half, :], preferred_element_type=F32)
        acc = acc + jnp.dot(yb_ref[rs, :].astype(BF16), wbf_ref[half:, :], preferred_element_type=F32)
        x = x_ref[rs, :] + mod_rows(gt_ref, rs) * acc
        r = lax.rsqrt(jnp.mean(x * x, axis=-1, keepdims=True) + EPS)
        if final:
            o_ref = rest[0]
            o_ref[rs, :] = (x * r) * g_ref[...]
        else:
            sc_ref, sh_ref, x_out_ref, h_ref = rest[:4]
            x_out_ref[rs, :] = x
            h = (x * r) * g_ref[...] * (1.0 + mod_rows(sc_ref, rs)) + mod_rows(sh_ref, rs)
            h_ref[rs, :] = h.astype(h_ref.dtype)


def _out_proj(ya, yb, ya_col, yb_col, w, x, mod, g, next_mod, tm):
    rows = x.shape[0]
    half = D_MODEL // 2
    final = next_mod is None
    row_f32 = pl.BlockSpec((tm, D_MODEL), lambda i: (i, 0))
    in_specs = [
        pl.BlockSpec((tm, half), lambda i: (i, ya_col)),
        pl.BlockSpec((tm, half), lambda i: (i, yb_col)),
        pl.BlockSpec((D_MODEL, D_MODEL), lambda i: (0, 0), pipeline_mode=pl.Buffered(1)),
        row_f32,
        mod.spec(2, tm),
        pl.BlockSpec((1, D_MODEL), lambda i: (0, 0)),
    ]
    args = [ya, yb, w, x, mod.arr, g.reshape(1, D_MODEL)]
    if final:
        out_specs, out_shape = row_f32, jax.ShapeDtypeStruct((rows, D_MODEL), F32)
    else:
        in_specs += [next_mod.spec(1, tm), next_mod.spec(0, tm)]
        args += [next_mod.arr, next_mod.arr]
        out_specs = [row_f32, row_f32]
        out_shape = [jax.ShapeDtypeStruct((rows, D_MODEL), F32), jax.ShapeDtypeStruct((rows, D_MODEL), BF16)]
    return pl.pallas_call(
        functools.partial(_out_proj_kernel, final=final),
        grid=(rows // tm,),
        in_specs=in_specs,
        out_specs=out_specs,
        out_shape=out_shape,
        scratch_shapes=[pltpu.VMEM((D_MODEL, D_MODEL), BF16)],
        compiler_params=_params("arbitrary"),
        name="out_proj_final" if final else "out_proj",
    )(*args)


def _conv_proj_kernel(h_ref, wa_ref, wb_ref, wz_ref, glu_ref, sz_ref, wbf_ref):
    @pl.when(pl.program_id(1) == 0)
    def _():
        wbf_ref[0] = wa_ref[...].astype(BF16)
        wbf_ref[1] = wb_ref[...].astype(BF16)
        wbf_ref[2] = wz_ref[...].astype(BF16)

    h = h_ref[...]
    a = jnp.dot(h, wbf_ref[0], preferred_element_type=F32)
    b = jnp.dot(h, wbf_ref[1], preferred_element_type=F32)
    glu_ref[...] = a * _sigmoid(b)
    sz_ref[...] = _silu(jnp.dot(h, wbf_ref[2], preferred_element_type=F32))


def _conv_proj(h, w, tm):
    rows = h.shape[0]
    tn = 512
    nj = D_C // tn
    out = pl.BlockSpec((tm, tn), lambda j, i: (i, j))
    return pl.pallas_call(
        _conv_proj_kernel,
        grid=(nj, rows // tm),
        in_specs=[
            pl.BlockSpec((tm, D_MODEL), lambda j, i: (i, 0)),
            pl.BlockSpec((D_MODEL, tn), lambda j, i: (0, j)),
            pl.BlockSpec((D_MODEL, tn), lambda j, i: (0, nj + j)),
            pl.BlockSpec((D_MODEL, tn), lambda j, i: (0, 2 * nj + j)),
        ],
        out_specs=[out, out],
        out_shape=[jax.ShapeDtypeStruct((rows, D_C), F32)] * 2,
        scratch_shapes=[pltpu.VMEM((3, D_MODEL, tn), BF16)],
        compiler_params=_params("arbitrary", "arbitrary"),
        name="conv_proj",
    )(h, w, w, w)


HALO = 32
CONV_LANES = 128
LN_ROWS = 16


def _conv_ln_kernel(glu_ref, halo_ref, sz_ref, w_ref, cb_ref, g_ref, b_ref, o_ref, buf_scr, y_scr, wb_scr,
                    *, tiles_per_seq, tm):
    first = (pl.program_id(0) % tiles_per_seq) == 0

    @pl.when(pl.program_id(0) == 0)
    def _():
        for k in range(CONV_W):
            wb_scr[k] = jnp.broadcast_to(w_ref[k:k + 1, :], (SUBLANES, D_C))

    @pl.when(first)
    def _():
        buf_scr[0:HALO, :] = jnp.zeros((HALO, D_C), F32)

    @pl.when(jnp.logical_not(first))
    def _():
        buf_scr[0:HALO, :] = halo_ref[...]

    buf_scr[HALO:HALO + tm, :] = glu_ref[...]

    off = HALO - (CONV_W - 1)
    taps = [[(a, SUBLANES * a + r - off) for a in range((CONV_W + off) // SUBLANES + 1)
             if 0 <= SUBLANES * a + r - off < CONV_W] for r in range(SUBLANES)]
    rowid = lax.broadcasted_iota(jnp.int32, (SUBLANES, CONV_LANES), 0)

    for lc in range(D_C // CONV_LANES):
        lanes = slice(lc * CONV_LANES, (lc + 1) * CONV_LANES)

        def z_block(u0, r):
            acc = None
            for a, k in taps[r]:
                term = wb_scr[k, :, lanes] * buf_scr[pl.ds(u0 + SUBLANES * a, SUBLANES), lanes]
                acc = term if acc is None else acc + term
            return acc

        def body(blk, carry):
            t0 = pl.multiple_of(blk * SUBLANES, SUBLANES)
            terms = [z_block(t0, 0)]
            nxt = []
            for r in range(1, SUBLANES):
                zn = z_block(t0 + SUBLANES, r)
                mixed = jnp.where(rowid >= r, carry[r - 1], zn)
                terms.append(pltpu.roll(mixed, SUBLANES - r, 0))
                nxt.append(zn)
            while len(terms) > 1:
                terms = [terms[i] + terms[i + 1] for i in range(0, len(terms), 2)]
            y_scr[pl.ds(t0, SUBLANES), lanes] = terms[0]
            return tuple(nxt)

        init = tuple(z_block(0, r) for r in range(1, SUBLANES))
        lax.fori_loop(0, tm // SUBLANES, body, init, unroll=4)

    def norm_rows(blk, carry):
        r0 = pl.multiple_of(blk * LN_ROWS, LN_ROWS)
        rs = pl.ds(r0, LN_ROWS)
        y = _layernorm_rows(y_scr[rs, :] + cb_ref[...], g_ref[...], b_ref[...])
        o_ref[rs, :] = (_silu(y) * sz_ref[rs, :]).astype(o_ref.dtype)
        return carry

    lax.fori_loop(0, tm // LN_ROWS, norm_rows, 0, unroll=8)


def _conv_ln(glu, sz, conv_w, conv_b, ln_g, ln_b, seq, tm):
    rows = glu.shape[0]
    vec = pl.BlockSpec((1, D_C), lambda i: (0, 0))
    tile = pl.BlockSpec((tm, D_C), lambda i: (i, 0))
    return pl.pallas_call(
        functools.partial(_conv_ln_kernel, tiles_per_seq=seq // tm, tm=tm),
        grid=(rows // tm,),
        in_specs=[tile,
                  pl.BlockSpec((HALO, D_C), lambda i: (jnp.maximum(i * (tm // HALO) - 1, 0), 0)),
                  tile,
                  pl.BlockSpec((CONV_W, D_C), lambda i: (0, 0)),
                  vec, vec, vec],
        out_specs=tile,
        out_shape=jax.ShapeDtypeStruct((rows, D_C), BF16),
        scratch_shapes=[pltpu.VMEM((HALO + tm, D_C), F32), pltpu.VMEM((tm, D_C), F32),
                        pltpu.VMEM((CONV_W, SUBLANES, D_C), F32)],
        compiler_params=_params("arbitrary"),
        name="conv_ln",
    )(glu, glu, sz, conv_w, conv_b.reshape(1, D_C), ln_g.reshape(1, D_C), ln_b.reshape(1, D_C))


def _conv_ln_sample_kernel(st_ref, glu_ref, sz_ref, w_ref, cb_ref, g_ref, b_ref, o_ref, ns_ref, buf_scr,
                           *, n_new):
    hist = CONV_W - 1
    buf_scr[0:hist, :] = st_ref[0]
    buf_scr[hist:hist + n_new, :] = glu_ref[0]
    acc = jnp.zeros((n_new, D_C), F32) + cb_ref[...]
    for k in range(CONV_W):
        acc = acc + w_ref[k:k + 1, :] * buf_scr[k:k + n_new, :]
    y = _layernorm_rows(acc, g_ref[...], b_ref[...])
    o_ref[0] = _silu(y) * sz_ref[0]
    ns_ref[0] = buf_scr[n_new:n_new + hist, :]


def _conv_ln_sample(state, glu, sz, conv_w, conv_b, ln_g, ln_b, n_seq, n_new):
    hist = CONV_W - 1
    vec = pl.BlockSpec((1, D_C), lambda s: (0, 0))
    new = pl.BlockSpec((1, n_new, D_C), lambda s: (s, 0, 0))
    st = pl.BlockSpec((1, hist, D_C), lambda s: (s, 0, 0))
    y, ns = pl.pallas_call(
        functools.partial(_conv_ln_sample_kernel, n_new=n_new),
        grid=(n_seq,),
        in_specs=[st, new, new, pl.BlockSpec((CONV_W, D_C), lambda s: (0, 0)), vec, vec, vec],
        out_specs=[new, st],
        out_shape=[jax.ShapeDtypeStruct((n_seq, n_new, D_C), F32),
                   jax.ShapeDtypeStruct((n_seq, hist, D_C), F32)],
        scratch_shapes=[pltpu.VMEM((hist + n_new, D_C), F32)],
        compiler_params=_params("arbitrary"),
        name="conv_ln_sample",
    )(state, glu.reshape(n_seq, n_new, D_C), sz.reshape(n_seq, n_new, D_C), conv_w,
      conv_b.reshape(1, D_C), ln_g.reshape(1, D_C), ln_b.reshape(1, D_C))
    return y.reshape(n_seq * n_new, D_C), ns


def _trunk(x, mods, tm, seq, weights, attend, conv):
    (norm_g, ab_w_in, a_ln_g, a_ln_b, a_ws, a_bs, ab_w_out, c_w_in, c_w_out, final_norm_g) = weights
    tm_row = min(tm, ROW_TILE)
    u, h = _proj_norm(x, norm_g[0], mods[0], ab_w_in[0], tm)
    seg = lambda s, ep: _proj_seg(h, ab_w_in[0], s, ep, a_ln_g[0], a_ln_b[0], tm)
    vn = seg(1, "gelu_ln")
    q, zb = seg(3, "none"), seg(6, "silu")
    (k, k_heads), (v, v_heads) = seg(4, "heads"), seg(5, "heads")
    ya = _a_mix(h, ab_w_in[0], 2, u, vn, a_ws[0], a_bs[0], seq, tm)
    yb = attend(q, k, v, zb)
    x1, h1 = _out_proj(ya, yb, 0, 0, ab_w_out[0], x, mods[0], norm_g[1], mods[1], tm_row)
    glu, sz = _conv_proj(h1, c_w_in[0], tm)
    y2, conv_state = conv(glu, sz)
    y = _out_proj(y2, y2, 0, 1, c_w_out[0], x1, mods[1], final_norm_g, None, tm_row)
    return y, k_heads, v_heads, vn, conv_state


def kernel(x_prompt, x_sample, cache_b_k, cache_b_v, state_c_conv, page_table, c_prompt, c_sample,
           ada_w, ada_b, norm_g, ab_w_in, a_ln_g, a_ln_b, a_ws, a_bs, ab_w_out,
           c_w_in, c_conv_w, c_conv_b, c_ln_g, c_ln_b, c_w_out, final_norm_g):
    batch, seq, _ = x_prompt.shape
    n_seq, n_new, _ = x_sample.shape
    weights = (norm_g, ab_w_in, a_ln_g, a_ln_b, a_ws, a_bs, ab_w_out, c_w_in, c_w_out, final_norm_g)

    mod = _adaln(jnp.concatenate([c_prompt, c_sample], axis=0), ada_w, ada_b)
    mod = mod.reshape(DEPTH, batch + n_seq, 3, D_MODEL)
    tm_p = PROJ_TILE
    mods_p, mods_s = [], []
    for l in range(DEPTH):
        mods_p.append(_Mod(mod[l, :batch].reshape(batch * 3, 1, D_MODEL), seq))
        per_row = jnp.repeat(mod[l, batch:], n_new, axis=0).transpose(1, 0, 2)
        mods_s.append(_Mod(per_row))

    yp, kp, vp, _, glu_p = _trunk(
        x_prompt.reshape(batch * seq, D_MODEL), mods_p, tm_p, seq, weights,
        attend=lambda q, k, v, zb: _moba_prompt(q, k, v, zb, batch, seq),
        conv=lambda glu, sz: (_conv_ln(glu, sz, c_conv_w[0], c_conv_b[0], c_ln_g[0], c_ln_b[0], seq, ROW_TILE), glu))
    ys, ks, vs, vas, conv_s = _trunk(
        x_sample.reshape(n_seq * n_new, D_MODEL), mods_s, n_seq * n_new, n_new, weights,
        attend=lambda q, k, v, zb: _moba_sample(q, k, v, zb, cache_b_k[0], cache_b_v[0], page_table,
                                               n_seq, n_new),
        conv=lambda glu, sz: _conv_ln_sample(state_c_conv[0], glu, sz, c_conv_w[0], c_conv_b[0],
                                             c_ln_g[0], c_ln_b[0], n_seq, n_new))

    hd = (B_HEADS, B_HEAD_DIM)
    conv_p = glu_p.reshape(batch, seq, D_C)[:, seq - (CONV_W - 1):, :]
    return (yp.reshape(batch, seq, D_MODEL), ys.reshape(n_seq, n_new, D_MODEL),
            kp.reshape(1, batch, seq, *hd), vp.reshape(1, batch, seq, *hd),
            ks.reshape(1, n_seq, n_new, *hd), vs.reshape(1, n_seq, n_new, *hd),
            vas.reshape(1, n_seq, n_new, D_A), conv_p[None], conv_s[None])
```
